```python
import jax, jax.numpy as jnp
from jax import lax
import numpy as np

D_MODEL = 4096
BATCH = 1
SEQ = 16384
DEPTH = 1

CONV_WIDTH = D_MODEL // 2
CONV_KERNEL = 31
N_ATTN_HEADS = 16
HEAD_DIM = 128
ATTN_WIDTH = N_ATTN_HEADS * HEAD_DIM
N_IDX_HEADS = 32
IDX_HEAD_DIM = 128
IDX_ROPE_DIM = 64
TOPK_MAX = 256
Q_BLOCK = 128
ROPE_THETA = 10000.0
N_BRANCHES = 2
MIX_WIDTH = CONV_WIDTH + ATTN_WIDTH
PEER_HEADS = 8
PEER_N_KEYS = 128
PEER_N_EXPERTS = PEER_N_KEYS * PEER_N_KEYS
PEER_HALF = 128
PEER_QUERY_DIM = 2 * PEER_HALF
PEER_TOPK = 16
PEER_BLOCK = 128
N_ADA = 6
EPS = 1e-6

IN_SIZES = (2 * CONV_WIDTH, ATTN_WIDTH, ATTN_WIDTH, ATTN_WIDTH,
            N_IDX_HEADS * IDX_HEAD_DIM, IDX_HEAD_DIM, N_IDX_HEADS, N_BRANCHES * D_MODEL)
IN_TOTAL = int(sum(IN_SIZES))
IN_OFFSETS = tuple(int(o) for o in np.cumsum(IN_SIZES)[:-1])

kernel_name = "hybrid_conv_dsa_peer_block"


def rms_norm(x, g):
    xf = x.astype(jnp.float32)
    y = xf * lax.rsqrt(jnp.mean(xf * xf, axis=-1, keepdims=True) + EPS)
    return (y * g.astype(jnp.float32)).astype(x.dtype)


def layer_norm(x, g, b):
    xf = x.astype(jnp.float32)
    mu = jnp.mean(xf, axis=-1, keepdims=True)
    var = jnp.mean(jnp.square(xf - mu), axis=-1, keepdims=True)
    y = (xf - mu) * lax.rsqrt(var + EPS)
    return (y * g.astype(jnp.float32) + b.astype(jnp.float32)).astype(x.dtype)


def modulate(h, shift, scale):
    return h * (1.0 + scale[:, None, :]) + shift[:, None, :]


def rope(x, positions):
    d = x.shape[-1]
    inv_freq = ROPE_THETA ** (-jnp.arange(0, d, 2, dtype=jnp.float32) / d)
    ang = positions.astype(jnp.float32)[..., None] * inv_freq
    cos = jnp.cos(ang)[:, :, None, :]
    sin = jnp.sin(ang)[:, :, None, :]
    xf = x.astype(jnp.float32)
    x1, x2 = xf[..., : d // 2], xf[..., d // 2:]
    out = jnp.concatenate([x1 * cos - x2 * sin, x2 * cos + x1 * sin], axis=-1)
    return out.astype(x.dtype)


def rope_partial(x, positions, rot_dim):
    return jnp.concatenate([rope(x[..., :rot_dim], positions), x[..., rot_dim:]], axis=-1)


def conformer_conv(u_glu, conv_w, conv_b, ln_g, ln_b):
    a, g = jnp.split(u_glu, 2, axis=-1)
    u = a * jax.nn.sigmoid(g)
    u_pad = jnp.pad(u, ((0, 0), (CONV_KERNEL - 1, 0), (0, 0)))
    y = lax.conv_general_dilated(
        u_pad, conv_w[:, None, :].astype(u.dtype), window_strides=(1,), padding="VALID",
        dimension_numbers=("NWC", "WIO", "NWC"), feature_group_count=CONV_WIDTH)
    y = y + conv_b
    return jax.nn.silu(layer_norm(y, ln_g, ln_b))


def dsa_attention(q, k, v, q_idx, k_idx, w_idx, positions):
    B, S, H, Dh = q.shape
    k_sel = min(TOPK_MAX, S // 4)
    nblk = S // Q_BLOCK

    def to_blocks(t):
        return t.reshape(B, nblk, Q_BLOCK, *t.shape[2:]).swapaxes(0, 1)

    def block_fn(args):
        qb, qib, wb, pb = args
        logits = jnp.einsum("bqhd,bsd->bqhs", qib, k_idx).astype(jnp.float32)
        score = jnp.einsum("bqh,bqhs->bqs", wb.astype(jnp.float32), jax.nn.relu(logits))
        causal = positions[:, None, :] <= pb[:, :, None]
        score = jnp.where(causal, score, -jnp.inf)
        top_s, idx = lax.top_k(score, k_sel)
        valid = jnp.isfinite(top_s)
        kg = jax.vmap(lambda kk, ii: kk[ii])(k, idx)
        vg = jax.vmap(lambda vv, ii: vv[ii])(v, idx)
        att = jnp.einsum("bqhd,bqkhd->bhqk", qb, kg).astype(jnp.float32) * (HEAD_DIM ** -0.5)
        att = jnp.where(valid[:, None, :, :], att, -jnp.inf)
        p = jax.nn.softmax(att, axis=-1).astype(v.dtype)
        return jnp.einsum("bhqk,bqkhd->bqhd", p, vg)

    out = lax.map(block_fn, (to_blocks(q), to_blocks(q_idx), to_blocks(w_idx), to_blocks(positions)))
    return out.swapaxes(0, 1).reshape(B, S, H * Dh)


def peer(h, w_q, sub_keys, u_exp, v_exp):
    B, S, D = h.shape
    hb = h.reshape((B * S) // PEER_BLOCK, PEER_BLOCK, D)

    def block_fn(xb):
        T = xb.shape[0]
        q = (xb @ w_q).reshape(T, PEER_HEADS, 2, PEER_HALF)
        s = jnp.einsum("thcd,hckd->thck", q, sub_keys).astype(jnp.float32)
        s_top, i_top = lax.top_k(s, PEER_TOPK)
        cand = (s_top[:, :, 0, :, None] + s_top[:, :, 1, None, :]).reshape(T, PEER_HEADS, PEER_TOPK * PEER_TOPK)
        cand_idx = (i_top[:, :, 0, :, None] * PEER_N_KEYS + i_top[:, :, 1, None, :]).reshape(T, PEER_HEADS, PEER_TOPK * PEER_TOPK)
        best, pos = lax.top_k(cand, PEER_TOPK)
        expert = jnp.take_along_axis(cand_idx, pos, axis=-1)
        g = jax.nn.softmax(best, axis=-1)
        u = u_exp[expert]
        act = jax.nn.gelu(jnp.einsum("td,thkd->thk", xb, u).astype(jnp.float32), approximate=False)
        v = v_exp[expert]
        return jnp.einsum("thk,thkd->td", (g * act).astype(xb.dtype), v)

    return lax.map(block_fn, hb).reshape(B, S, D)


def setup_inputs(seed: int = 0) -> dict:
    key = jax.random.key(seed)
    ks = jax.random.split(key, 20)
    f32 = jnp.float32
    nrm = lambda k, shape, std: jax.random.normal(k, shape, f32) * std
    x = nrm(ks[0], (BATCH, SEQ, D_MODEL), 1.0)
    c = nrm(ks[1], (BATCH, D_MODEL), 1.0)
    positions = jnp.broadcast_to(jnp.arange(SEQ, dtype=jnp.int32), (BATCH, SEQ))
    w_ada = nrm(ks[2], (DEPTH, D_MODEL, N_ADA * D_MODEL), 0.5 * D_MODEL ** -0.5)
    b_ada = nrm(ks[3], (DEPTH, N_ADA * D_MODEL), 0.01)
    norm_mix_g = 1.0 + nrm(ks[4], (DEPTH, D_MODEL), 0.02)
    norm_ffn_g = 1.0 + nrm(ks[5], (DEPTH, D_MODEL), 0.02)
    w_in = nrm(ks[6], (DEPTH, D_MODEL, IN_TOTAL), D_MODEL ** -0.5)
    conv_w = nrm(ks[7], (DEPTH, CONV_KERNEL, CONV_WIDTH), CONV_KERNEL ** -0.5)
    conv_b = nrm(ks[8], (DEPTH, CONV_WIDTH), 0.01)
    conv_ln_g = 1.0 + nrm(ks[9], (DEPTH, CONV_WIDTH), 0.02)
    conv_ln_b = nrm(ks[10], (DEPTH, CONV_WIDTH), 0.01)
    k_idx_ln_g = 1.0 + nrm(ks[11], (DEPTH, IDX_HEAD_DIM), 0.02)
    k_idx_ln_b = nrm(ks[12], (DEPTH, IDX_HEAD_DIM), 0.01)
    w_out = nrm(ks[13], (DEPTH, MIX_WIDTH, D_MODEL), CONV_WIDTH ** -0.5)
    peer_w_q = nrm(ks[14], (DEPTH, D_MODEL, PEER_HEADS * PEER_QUERY_DIM), D_MODEL ** -0.5)
    peer_sub_keys = nrm(ks[15], (DEPTH, PEER_HEADS, 2, PEER_N_KEYS, PEER_HALF), PEER_HALF ** -0.5)
    peer_u = nrm(ks[16], (DEPTH, PEER_N_EXPERTS, D_MODEL), D_MODEL ** -0.5)
    peer_v = nrm(ks[17], (DEPTH, PEER_N_EXPERTS, D_MODEL), (PEER_HEADS * PEER_TOPK) ** -0.5)
    final_norm_g = 1.0 + nrm(ks[18], (D_MODEL,), 0.02)
    return {"x": x, "c": c, "positions": positions, "w_ada": w_ada, "b_ada": b_ada,
            "norm_mix_g": norm_mix_g, "norm_ffn_g": norm_ffn_g, "w_in": w_in,
            "conv_w": conv_w, "conv_b": conv_b, "conv_ln_g": conv_ln_g, "conv_ln_b": conv_ln_b,
            "k_idx_ln_g": k_idx_ln_g, "k_idx_ln_b": k_idx_ln_b, "w_out": w_out,
            "peer_w_q": peer_w_q, "peer_sub_keys": peer_sub_keys, "peer_u": peer_u, "peer_v": peer_v,
            "final_norm_g": final_norm_g}


def reference(x, c, positions, w_ada, b_ada, norm_mix_g, norm_ffn_g, w_in, conv_w, conv_b,
              conv_ln_g, conv_ln_b, k_idx_ln_g, k_idx_ln_b, w_out, peer_w_q, peer_sub_keys,
              peer_u, peer_v, final_norm_g):
    B, S, D = x.shape
    c_act = jax.nn.silu(c)
    for l in range(DEPTH):
        ada = c_act @ w_ada[l] + b_ada[l]
        shift1, scale1, gate1, shift2, scale2, gate2 = jnp.split(ada, N_ADA, axis=-1)

        h = modulate(rms_norm(x, norm_mix_g[l]), shift1, scale1)
        proj = h @ w_in[l]
        u_glu, q, k, v, q_idx, k_idx, w_idx, gate_logits = jnp.split(proj, IN_OFFSETS, axis=-1)

        a_out = conformer_conv(u_glu, conv_w[l], conv_b[l], conv_ln_g[l], conv_ln_b[l])

        q = rope(q.reshape(B, S, N_ATTN_HEADS, HEAD_DIM), positions)
        k = rope(k.reshape(B, S, N_ATTN_HEADS, HEAD_DIM), positions)
        v = v.reshape(B, S, N_ATTN_HEADS, HEAD_DIM)
        q_idx = rope_partial(q_idx.reshape(B, S, N_IDX_HEADS, IDX_HEAD_DIM), positions, IDX_ROPE_DIM)
        k_idx = layer_norm(k_idx, k_idx_ln_g[l], k_idx_ln_b[l])
        k_idx = rope_partial(k_idx[:, :, None, :], positions, IDX_ROPE_DIM)[:, :, 0, :]
        w_idx = w_idx * (N_IDX_HEADS ** -0.5 * IDX_HEAD_DIM ** -0.5)
        b_out = dsa_attention(q, k, v, q_idx, k_idx, w_idx, positions)

        gates = jax.nn.sigmoid(gate_logits).reshape(B, S, N_BRANCHES, D)
        y = (gates[:, :, 0, :] * (a_out @ w_out[l][:CONV_WIDTH])
             + gates[:, :, 1, :] * (b_out @ w_out[l][CONV_WIDTH:]))
        x = x + gate1[:, None, :] * y

        h2 = modulate(rms_norm(x, norm_ffn_g[l]), shift2, scale2)
        x = x + gate2[:, None, :] * peer(h2, peer_w_q[l], peer_sub_keys[l], peer_u[l], peer_v[l])
    return rms_norm(x, final_norm_g)
```

```python
import functools

import numpy as np
import jax
import jax.numpy as jnp
from jax import lax
from jax.experimental import pallas as pl
from jax.experimental.pallas import tpu as pltpu

N_ATTN_HEADS = 16
HEAD_DIM = 128
N_IDX_HEADS = 32
IDX_HEAD_DIM = 128
IDX_ROPE_DIM = 64
TOPK_MAX = 256
ROPE_THETA = 10000.0
PEER_TOPK = 16
EPS = 1e-6

LANE = 128
SUBLANE = 8
VMEM_LIMIT = 56 * 1024 * 1024
NEG_BIG = -1e30
INT_MIN = -2 ** 31
INT_MAX = 2 ** 31 - 1

TM_NORM = 512
TM_MM = 1024
TN_MM = 512
TM_CONV = 256
TQ_DSA = 256
TK_DSA = 512
TM_MERGE = 512
TN_MERGE = 256
TT_ROUTE = 512
TT_PEER = 512
EC_PEER = 256
TM_FINAL = 256

f32 = jnp.float32
bf16 = jnp.bfloat16
i32 = jnp.int32

_NT = (((1,), (1,)), ((), ()))
_TN = (((0,), (0,)), ((), ()))


def _cp(sem, vmem=VMEM_LIMIT):
    return pltpu.CompilerParams(dimension_semantics=sem, vmem_limit_bytes=vmem)


def _tile(n, t):
    t = min(n, t)
    assert n % t == 0, (n, t)
    return t


def _ada_kernel(c_ref, w_ref, b_ref, o_ref):
    c = c_ref[...]
    ca = (c * jax.nn.sigmoid(c)).astype(bf16)
    o_ref[...] = jnp.dot(ca, w_ref[...].astype(bf16), preferred_element_type=f32) + b_ref[...]


def _ada(c8, w, b):
    d, n = w.shape
    tn = _tile(n, 512)
    return pl.pallas_call(
        _ada_kernel,
        grid=(n // tn,),
        in_specs=[pl.BlockSpec((SUBLANE, d), lambda j: (0, 0)),
                  pl.BlockSpec((d, tn), lambda j: (0, j)),
                  pl.BlockSpec((1, tn), lambda j: (0, j))],
        out_specs=pl.BlockSpec((SUBLANE, tn), lambda j: (0, j)),
        out_shape=jax.ShapeDtypeStruct((SUBLANE, n), f32),
        compiler_params=_cp(("arbitrary",)),
        name="ada",
    )(c8, w, b)


def _norm_mod_kernel(x_ref, g_ref, sc_ref, sh_ref, o_ref):
    x = x_ref[...]
    y = x * lax.rsqrt(jnp.mean(x * x, axis=-1, keepdims=True) + EPS) * g_ref[...]
    o_ref[...] = (y * (1.0 + sc_ref[...]) + sh_ref[...]).astype(o_ref.dtype)


def _norm_mod(x, g, scale, shift):
    s, d = x.shape
    tm = _tile(s, TM_NORM)
    row = pl.BlockSpec((1, d), lambda i: (0, 0))
    return pl.pallas_call(
        _norm_mod_kernel,
        grid=(s // tm,),
        in_specs=[pl.BlockSpec((tm, d), lambda i: (i, 0)), row, row, row],
        out_specs=pl.BlockSpec((tm, d), lambda i: (i, 0)),
        out_shape=jax.ShapeDtypeStruct((s, d), bf16),
        compiler_params=_cp(("arbitrary",)),
        name="norm_mod",
    )(x, g, scale, shift)


def _glu_kernel(h_ref, wa_ref, wg_ref, o_ref):
    h = h_ref[...]
    a = jnp.dot(h, wa_ref[...], preferred_element_type=f32)
    g = jnp.dot(h, wg_ref[...], preferred_element_type=f32)
    o_ref[...] = a * jax.nn.sigmoid(g)


def _glu(h, w_glu, cw):
    s, d = h.shape
    tm, tn = _tile(s, TM_MM), _tile(cw, TN_MM)
    nj = cw // tn
    return pl.pallas_call(
        _glu_kernel,
        grid=(s // tm, nj),
        in_specs=[pl.BlockSpec((tm, d), lambda i, j: (i, 0)),
                  pl.BlockSpec((d, tn), lambda i, j: (0, j)),
                  pl.BlockSpec((d, tn), lambda i, j: (0, j + nj))],
        out_specs=pl.BlockSpec((tm, tn), lambda i, j: (i, j)),
        out_shape=jax.ShapeDtypeStruct((s, cw), f32),
        compiler_params=_cp(("arbitrary", "arbitrary")),
        name="glu",
    )(h, w_glu, w_glu)


def _conv_kernel(prev_ref, cur_ref, w_ref, cb_ref, g_ref, b_ref, o_ref, ubuf, ybuf, *, tm, kw, halo, rs):
    i = pl.program_id(0)
    cw = cur_ref.shape[1]
    ubuf[0:halo, :] = jnp.where(i > 0, prev_ref[...], 0.0)
    ubuf[halo:, :] = cur_ref[...]
    off = halo - (kw - 1)

    def row_body(r, carry):
        r0 = pl.multiple_of(r * rs, rs)
        for c in range(cw // LANE):
            cs = slice(c * LANE, (c + 1) * LANE)
            acc = jnp.zeros((rs, LANE), f32) + cb_ref[:, cs]
            win = ubuf[pl.ds(r0, rs + halo), cs]
            for k in range(kw):
                acc = acc + w_ref[k:k + 1, cs] * win[off + k:off + k + rs]
            ybuf[pl.ds(r0, rs), cs] = acc
        return carry

    lax.fori_loop(0, tm // rs, row_body, 0)
    y = ybuf[...]
    mu = jnp.mean(y, axis=-1, keepdims=True)
    yc = y - mu
    var = jnp.mean(yc * yc, axis=-1, keepdims=True)
    z = yc * lax.rsqrt(var + EPS) * g_ref[...] + b_ref[...]
    o_ref[...] = (z * jax.nn.sigmoid(z)).astype(o_ref.dtype)


def _conv(u, conv_w, conv_b, ln_g, ln_b):
    s, cw = u.shape
    kw = conv_w.shape[0]
    tm = _tile(s, TM_CONV)
    halo = -(-(kw - 1) // SUBLANE) * SUBLANE
    rs = 32
    assert tm % halo == 0 and tm % rs == 0
    kwp = -(-kw // SUBLANE) * SUBLANE
    w_pad = jnp.zeros((kwp, cw), f32).at[:kw].set(conv_w)
    row = pl.BlockSpec((1, cw), lambda i: (0, 0))
    return pl.pallas_call(
        functools.partial(_conv_kernel, tm=tm, kw=kw, halo=halo, rs=rs),
        grid=(s // tm,),
        in_specs=[pl.BlockSpec((halo, cw), lambda i: (jnp.maximum(i * (tm // halo) - 1, 0), 0)),
                  pl.BlockSpec((tm, cw), lambda i: (i, 0)),
                  pl.BlockSpec((kwp, cw), lambda i: (0, 0)), row, row, row],
        out_specs=pl.BlockSpec((tm, cw), lambda i: (i, 0)),
        out_shape=jax.ShapeDtypeStruct((s, cw), bf16),
        scratch_shapes=[pltpu.VMEM((halo + tm, cw), f32), pltpu.VMEM((tm, cw), f32)],
        compiler_params=_cp(("arbitrary",)),
        name="conv",
    )(u, u, w_pad, conv_b, ln_g, ln_b)


def _proj_t_kernel(w_ref, h_ref, cf_ref, sf_ref, cp_ref, sp_ref, o_ref, *, n_idx_tiles, n_rope_tiles, tn):
    j = pl.program_id(1)
    acc = lax.dot_general(w_ref[...], h_ref[...], _NT, preferred_element_type=f32)

    def rope_rows(cos_ref, sin_ref, rot):
        half = rot // 2
        cos, sin = cos_ref[...], sin_ref[...]
        for hd in range(tn // LANE):
            r0 = hd * LANE
            x1 = acc[r0:r0 + half]
            x2 = acc[r0 + half:r0 + rot]
            o_ref[r0:r0 + half, :] = (x1 * cos - x2 * sin).astype(o_ref.dtype)
            o_ref[r0 + half:r0 + rot, :] = (x2 * cos + x1 * sin).astype(o_ref.dtype)
            if rot < LANE:
                o_ref[r0 + rot:r0 + LANE, :] = acc[r0 + rot:r0 + LANE].astype(o_ref.dtype)

    @pl.when(j < n_idx_tiles)
    def _():
        rope_rows(cp_ref, sp_ref, IDX_ROPE_DIM)

    @pl.when((j >= n_idx_tiles) & (j < n_idx_tiles + n_rope_tiles))
    def _():
        rope_rows(cf_ref, sf_ref, HEAD_DIM)

    @pl.when(j >= n_idx_tiles + n_rope_tiles)
    def _():
        o_ref[...] = acc.astype(o_ref.dtype)


def _proj_t(wt, h, cos_f, sin_f, cos_p, sin_p, n_idx_rows, n_rope_rows):
    n, d = wt.shape
    s = h.shape[0]
    tm, tn = _tile(s, TM_MM), _tile(n_rope_rows, TN_MM)
    assert n % tn == 0 and n_idx_rows % tn == 0 and n_rope_rows % tn == 0
    kern = functools.partial(_proj_t_kernel, n_idx_tiles=n_idx_rows // tn, n_rope_tiles=n_rope_rows // tn, tn=tn)
    tab = lambda t: pl.BlockSpec((t.shape[0], tm), lambda i, j: (0, i))
    return pl.pallas_call(
        kern,
        grid=(s // tm, n // tn),
        in_specs=[pl.BlockSpec((tn, d), lambda i, j: (j, 0)),
                  pl.BlockSpec((tm, d), lambda i, j: (i, 0)),
                  tab(cos_f), tab(sin_f), tab(cos_p), tab(sin_p)],
        out_specs=pl.BlockSpec((tn, tm), lambda i, j: (j, i)),
        out_shape=jax.ShapeDtypeStruct((n, s), bf16),
        compiler_params=_cp(("arbitrary", "arbitrary")),
        name="proj_t",
    )(wt, h, cos_f, sin_f, cos_p, sin_p)


def _proj_k_kernel(h_ref, w_ref, c_ref, s_ref, o_ref, *, tn):
    acc = jnp.dot(h_ref[...], w_ref[...], preferred_element_type=f32)
    cos, sin = c_ref[...], s_ref[...]
    for hd in range(tn // LANE):
        x = acc[:, hd * LANE:(hd + 1) * LANE]
        y = x * cos + pltpu.roll(x, HEAD_DIM // 2, 1) * sin
        o_ref[:, hd * LANE:(hd + 1) * LANE] = y.astype(o_ref.dtype)


def _proj_k(h, w, cos_tok, sin_tok):
    s, d = h.shape
    n = w.shape[1]
    tm, tn = _tile(s, TM_MM), _tile(n, TN_MM)
    tab = pl.BlockSpec((tm, LANE), lambda i, j: (i, 0))
    return pl.pallas_call(
        functools.partial(_proj_k_kernel, tn=tn),
        grid=(s // tm, n // tn),
        in_specs=[pl.BlockSpec((tm, d), lambda i, j: (i, 0)),
                  pl.BlockSpec((d, tn), lambda i, j: (0, j)), tab, tab],
        out_specs=pl.BlockSpec((tm, tn), lambda i, j: (i, j)),
        out_shape=jax.ShapeDtypeStruct((s, n), bf16),
        compiler_params=_cp(("arbitrary", "arbitrary")),
        name="proj_k",
    )(h, w, cos_tok, sin_tok)


def _kw_kernel(h_ref, w_ref, g_ref, b_ref, c_ref, s1_ref, s2_ref, kidx_ref, widx_ref, *, w_scale):
    acc = jnp.dot(h_ref[...], w_ref[...], preferred_element_type=f32)
    k = acc[:, :LANE]
    mu = jnp.mean(k, axis=-1, keepdims=True)
    kc = k - mu
    var = jnp.mean(kc * kc, axis=-1, keepdims=True)
    y = kc * lax.rsqrt(var + EPS) * g_ref[...] + b_ref[...]
    half = IDX_ROPE_DIM // 2
    y = y * c_ref[...] + pltpu.roll(y, half, 1) * s1_ref[...] + pltpu.roll(y, LANE - half, 1) * s2_ref[...]
    kidx_ref[...] = y.astype(kidx_ref.dtype)
    widx_ref[...] = (acc[:, LANE:] * w_scale).T


def _kw(h, w_kw, ln_g, ln_b, c_tok, s1_tok, s2_tok, w_scale):
    s, d = h.shape
    tm = _tile(s, TM_NORM)
    row = pl.BlockSpec((1, LANE), lambda i: (0, 0))
    tab = pl.BlockSpec((tm, LANE), lambda i: (i, 0))
    return pl.pallas_call(
        functools.partial(_kw_kernel, w_scale=w_scale),
        grid=(s // tm,),
        in_specs=[pl.BlockSpec((tm, d), lambda i: (i, 0)),
                  pl.BlockSpec((d, 2 * LANE), lambda i: (0, 0)), row, row, tab, tab, tab],
        out_specs=[pl.BlockSpec((tm, LANE), lambda i: (i, 0)),
                   pl.BlockSpec((LANE, tm), lambda i: (0, i))],
        out_shape=[jax.ShapeDtypeStruct((s, LANE), bf16), jax.ShapeDtypeStruct((LANE, s), f32)],
        compiler_params=_cp(("arbitrary",)),
        name="kw",
    )(h, w_kw, ln_g, ln_b, c_tok, s1_tok, s2_tok)


def _dsa_kernel(qidx_ref, q_ref, k_ref, v_ref, kidx_ref, w_ref, o_ref,
                key_ref, thr_ref, m_ref, l_ref, acc_ref, *, tq, tk, n_heads, n_idx_heads, k_sel, scale):
    i = pl.program_id(0)
    j = pl.program_id(1)
    last = ((i + 1) * tq - 1) // tk
    nkb = last + 1
    neg_key = int(np.array(-np.inf, np.float32).view(np.int32)) ^ 0x7FFFFFFF

    @pl.when(j == 0)
    def _score_and_select():
        m_ref[...] = jnp.full(m_ref.shape, NEG_BIG, f32)
        l_ref[...] = jnp.zeros(l_ref.shape, f32)
        acc_ref[...] = jnp.zeros(acc_ref.shape, f32)
        qpos = i * tq + lax.broadcasted_iota(i32, (tk, tq), 1)

        def score_body(c, carry):
            kc = kidx_ref[pl.ds(pl.multiple_of(c * tk, tk), tk), :]
            acc = jnp.zeros((tk, tq), f32)
            for h in range(n_idx_heads):
                lg = jnp.dot(kc, qidx_ref[h * LANE:(h + 1) * LANE, :], preferred_element_type=f32)
                acc = acc + w_ref[h:h + 1, :] * jnp.maximum(lg, 0.0)
            kpos = c * tk + lax.broadcasted_iota(i32, (tk, tq), 0)
            acc = jnp.where(kpos <= qpos, acc, -jnp.inf)
            bits = pltpu.bitcast(acc, i32)
            key_ref[c] = bits ^ ((bits >> 31) & 0x7FFFFFFF)
            return carry

        lax.fori_loop(0, nkb, score_body, 0)

        def bis_body(it, carry):
            lo, hi = carry
            mid = (lo >> 1) + (hi >> 1) + (lo & hi & 1)

            def cnt_body(c, cnt):
                parts = [cnt, jnp.zeros_like(cnt), jnp.zeros_like(cnt), jnp.zeros_like(cnt)]
                for g in range(tk // SUBLANE):
                    blk = key_ref[c, g * SUBLANE:(g + 1) * SUBLANE, :]
                    parts[g % 4] = parts[g % 4] + jnp.where(blk >= mid, 1.0, 0.0)
                return (parts[0] + parts[1]) + (parts[2] + parts[3])

            cnt = lax.fori_loop(0, nkb, cnt_body, jnp.zeros((SUBLANE, tq), f32))
            tot = jnp.broadcast_to(jnp.sum(cnt, axis=0, keepdims=True), (SUBLANE, tq))
            ge = tot >= float(k_sel)
            return jnp.where(ge, mid, lo), jnp.where(ge, hi, mid)

        lo0 = jnp.full((SUBLANE, tq), INT_MIN, i32)
        hi0 = jnp.full((SUBLANE, tq), INT_MAX, i32)
        lo, _ = lax.fori_loop(0, 32, bis_body, (lo0, hi0))
        thr_ref[...] = jnp.maximum(lo, neg_key + 1)

    @pl.when(j <= last)
    def _attend():
        bias = jnp.where(key_ref[j] >= thr_ref[0:1, :], 0.0, NEG_BIG)
        for h in range(n_heads):
            hs = slice(h * HEAD_DIM, (h + 1) * HEAD_DIM)
            st = jnp.dot(k_ref[:, hs], q_ref[hs, :], preferred_element_type=f32) * scale + bias
            m_old = m_ref[h]
            m_new = jnp.maximum(m_old, jnp.max(st, axis=0, keepdims=True))
            alpha = jnp.exp(m_old - m_new)
            p = jnp.exp(st - m_new[0:1, :])
            l_ref[h] = alpha * l_ref[h] + jnp.sum(p, axis=0, keepdims=True)
            pv = jnp.dot(v_ref[hs, :], p.astype(bf16), preferred_element_type=f32)
            acc_ref[hs, :] = alpha[0:1, :] * acc_ref[hs, :] + pv
            m_ref[h] = m_new

    @pl.when(j == last)
    def _finish():
        for h in range(n_heads):
            hs = slice(h * HEAD_DIM, (h + 1) * HEAD_DIM)
            out_t = acc_ref[hs, :] / l_ref[h][0:1, :]
            o_ref[:, hs] = out_t.T.astype(o_ref.dtype)


def _dsa(feat_t, k_tok, kidx, widx_t, n_idx_rows, attn_w, k_sel):
    s = k_tok.shape[0]
    tq, tk = _tile(s, TQ_DSA), _tile(s, TK_DSA)
    assert tk >= k_sel and n_idx_rows % attn_w == 0 and tk % tq == 0
    nq, nk = s // tq, s // tk
    qb = n_idx_rows // attn_w
    last = lambda i: ((i + 1) * tq - 1) // tk
    kern = functools.partial(_dsa_kernel, tq=tq, tk=tk, n_heads=attn_w // HEAD_DIM,
                             n_idx_heads=n_idx_rows // IDX_HEAD_DIM, k_sel=k_sel, scale=HEAD_DIM ** -0.5)
    return pl.pallas_call(
        kern,
        grid=(nq, nk),
        in_specs=[pl.BlockSpec((n_idx_rows, tq), lambda i, j: (0, i)),
                  pl.BlockSpec((attn_w, tq), lambda i, j: (qb, i)),
                  pl.BlockSpec((tk, attn_w), lambda i, j: (jnp.minimum(j, last(i)), 0)),
                  pl.BlockSpec((attn_w, tk), lambda i, j: (qb + 1, jnp.minimum(j, last(i)))),
                  pl.BlockSpec((s, LANE), lambda i, j: (0, 0)),
                  pl.BlockSpec((LANE, tq), lambda i, j: (0, i))],
        out_specs=pl.BlockSpec((tq, attn_w), lambda i, j: (i, 0)),
        out_shape=jax.ShapeDtypeStruct((s, attn_w), bf16),
        scratch_shapes=[pltpu.VMEM((nk, tk, tq), i32),
                        pltpu.VMEM((SUBLANE, tq), i32),
                        pltpu.VMEM((attn_w // HEAD_DIM, SUBLANE, tq), f32),
                        pltpu.VMEM((attn_w // HEAD_DIM, SUBLANE, tq), f32),
                        pltpu.VMEM((attn_w, tq), f32)],
        compiler_params=_cp(("arbitrary", "arbitrary")),
        name="dsa",
    )(feat_t, feat_t, k_tok, feat_t, kidx, widx_t)


def _merge_kernel(h_ref, a_ref, b_ref, wg0_ref, wg1_ref, wo0_ref, wo1_ref, x_ref, g_ref, o_ref):
    h = h_ref[...]
    g0 = jax.nn.sigmoid(jnp.dot(h, wg0_ref[...], preferred_element_type=f32))
    g1 = jax.nn.sigmoid(jnp.dot(h, wg1_ref[...], preferred_element_type=f32))
    ya = jnp.dot(a_ref[...], wo0_ref[...], preferred_element_type=f32)
    yb = jnp.dot(b_ref[...], wo1_ref[...], preferred_element_type=f32)
    o_ref[...] = x_ref[...] + g_ref[...] * (g0 * ya + g1 * yb)


def _merge(h, a_out, b_out, w_gate, w_out, x, gate1):
    s, d = x.shape
    cw = a_out.shape[1]
    assert b_out.shape[1] == cw
    tm, tn = _tile(s, TM_MERGE), _tile(d, TN_MERGE)
    nj = d // tn
    return pl.pallas_call(
        _merge_kernel,
        grid=(s // tm, nj),
        in_specs=[pl.BlockSpec((tm, d), lambda i, j: (i, 0)),
                  pl.BlockSpec((tm, cw), lambda i, j: (i, 0)),
                  pl.BlockSpec((tm, cw), lambda i, j: (i, 0)),
                  pl.BlockSpec((d, tn), lambda i, j: (0, j)),
                  pl.BlockSpec((d, tn), lambda i, j: (0, j + nj)),
                  pl.BlockSpec((cw, tn), lambda i, j: (0, j)),
                  pl.BlockSpec((cw, tn), lambda i, j: (1, j)),
                  pl.BlockSpec((tm, tn), lambda i, j: (i, j)),
                  pl.BlockSpec((1, tn), lambda i, j: (0, j))],
        out_specs=pl.BlockSpec((tm, tn), lambda i, j: (i, j)),
        out_shape=jax.ShapeDtypeStruct((s, d), f32),
        compiler_params=_cp(("arbitrary", "arbitrary")),
        name="merge",
    )(h, a_out, b_out, w_gate, w_gate, w_out, w_out, x, gate1)


def _top_sorted(x, k):
    n = x.shape[0]
    rows = lax.broadcasted_iota(i32, x.shape, 0).astype(f32)
    cur, tops = x, []
    for _ in range(k):
        m = jnp.max(cur, axis=0, keepdims=True)
        first = jnp.min(jnp.where(cur == m, rows, float(n)), axis=0, keepdims=True)
        cur = jnp.where(rows == first, -jnp.inf, cur)
        tops.append(m)
    return tops


def _route_kernel(h_ref, wq_ref, sk_ref, a_ref, b_ref, ea_ref, eb_ref, thr_ref, *, half, topk):
    qt = lax.dot_general(wq_ref[...], h_ref[...], _NT, preferred_element_type=f32).astype(bf16)
    a = jnp.dot(sk_ref[0, 0], qt[:half], preferred_element_type=f32)
    b = jnp.dot(sk_ref[0, 1], qt[half:], preferred_element_type=f32)
    ta = _top_sorted(a, topk)
    tb = jnp.concatenate(_top_sorted(b, topk), axis=0)
    cand = jnp.concatenate([t + tb for t in ta], axis=0)
    best = _top_sorted(cand, topk)
    thr = best[-1]
    z = jnp.ones_like(thr)
    for t in best[1:]:
        z = z + jnp.exp(t - best[0])
    a_ref[0] = a
    b_ref[0] = b
    ea_ref[0] = jnp.exp(a - ta[0])
    eb_ref[0] = jnp.exp(b - tb[0:1]) / z
    thr_ref[0] = jnp.broadcast_to(thr, thr_ref.shape[1:])


def _route(h2, wq_t, sub_keys):
    s, d = h2.shape
    heads, _, n_keys, half = sub_keys.shape
    tt = _tile(s, TT_ROUTE)
    big = pl.BlockSpec((1, n_keys, tt), lambda t, h: (h, 0, t))
    shp = jax.ShapeDtypeStruct((heads, n_keys, s), f32)
    return pl.pallas_call(
        functools.partial(_route_kernel, half=half, topk=PEER_TOPK),
        grid=(s // tt, heads),
        in_specs=[pl.BlockSpec((tt, d), lambda t, h: (t, 0)),
                  pl.BlockSpec((2 * half, d), lambda t, h: (h, 0)),
                  pl.BlockSpec((1, 2, n_keys, half), lambda t, h: (h, 0, 0, 0))],
        out_specs=[big, big, big, big, pl.BlockSpec((1, SUBLANE, tt), lambda t, h: (h, 0, t))],
        out_shape=[shp, shp, shp, shp, jax.ShapeDtypeStruct((heads, SUBLANE, s), f32)],
        compiler_params=_cp(("arbitrary", "arbitrary")),
        name="route",
    )(h2, wq_t, sub_keys)


def _peer_kernel(h_ref, u_ref, v_ref, a_ref, b_ref, ea_ref, eb_ref, thr_ref, o_ref, *, ec, heads, n_keys):
    e = pl.program_id(1)

    @pl.when(e == 0)
    def _():
        o_ref[...] = jnp.zeros(o_ref.shape, f32)

    at = lax.dot_general(u_ref[...], h_ref[...], _NT, preferred_element_type=f32)
    act = 0.5 * at * (1.0 + lax.erf(at * np.float32(np.sqrt(0.5))))
    pieces = []
    for ii in range(ec // n_keys):
        row = e * (ec // n_keys) + ii
        g = jnp.zeros((n_keys, at.shape[1]), f32)
        for h in range(heads):
            val = a_ref[h, pl.ds(row, 1), :] + b_ref[h]
            g = g + jnp.where(val >= thr_ref[h, 0:1, :], ea_ref[h, pl.ds(row, 1), :] * eb_ref[h], 0.0)
        pieces.append((g * act[ii * n_keys:(ii + 1) * n_keys]).astype(bf16))
    wt = jnp.concatenate(pieces, axis=0)
    o_ref[...] += lax.dot_general(wt, v_ref[...], _TN, preferred_element_type=f32)


def _peer(h2, u, v, a, b, ea, eb, thr):
    s, d = h2.shape
    n_exp = u.shape[0]
    heads, n_keys, _ = a.shape
    tt, ec = _tile(s, TT_PEER), _tile(n_exp, EC_PEER)
    assert ec % n_keys == 0
    tab = pl.BlockSpec((heads, n_keys, tt), lambda t, e: (0, 0, t))
    return pl.pallas_call(
        functools.partial(_peer_kernel, ec=ec, heads=heads, n_keys=n_keys),
        grid=(s // tt, n_exp // ec),
        in_specs=[pl.BlockSpec((tt, d), lambda t, e: (t, 0)),
                  pl.BlockSpec((ec, d), lambda t, e: (e, 0)),
                  pl.BlockSpec((ec, d), lambda t, e: (e, 0)),
                  tab, tab, tab, tab,
                  pl.BlockSpec((heads, SUBLANE, tt), lambda t, e: (0, 0, t))],
        out_specs=pl.BlockSpec((tt, d), lambda t, e: (t, 0)),
        out_shape=jax.ShapeDtypeStruct((s, d), f32),
        compiler_params=_cp(("arbitrary", "arbitrary")),
        name="peer",
    )(h2, u, v, a, b, ea, eb, thr)


def _final_kernel(x_ref, p_ref, g2_ref, fg_ref, o_ref):
    x = x_ref[...] + g2_ref[...] * p_ref[...]
    o_ref[...] = x * lax.rsqrt(jnp.mean(x * x, axis=-1, keepdims=True) + EPS) * fg_ref[...]


def _final(x1, p, gate2, fg):
    s, d = x1.shape
    tm = _tile(s, TM_FINAL)
    row = pl.BlockSpec((1, d), lambda i: (0, 0))
    blk = pl.BlockSpec((tm, d), lambda i: (i, 0))
    return pl.pallas_call(
        _final_kernel,
        grid=(s // tm,),
        in_specs=[blk, blk, row, row],
        out_specs=blk,
        out_shape=jax.ShapeDtypeStruct((s, d), f32),
        compiler_params=_cp(("arbitrary",)),
        name="final",
    )(x1, p, gate2, fg)


def _rope_tables(positions, dim):
    inv_freq = ROPE_THETA ** (-jnp.arange(0, dim, 2, dtype=f32) / dim)
    ang = positions.astype(f32)[:, None] * inv_freq
    return jnp.cos(ang), jnp.sin(ang)


def kernel(x, c, positions, w_ada, b_ada, norm_mix_g, norm_ffn_g, w_in, conv_w, conv_b, conv_ln_g, conv_ln_b,
           k_idx_ln_g, k_idx_ln_b, w_out, peer_w_q, peer_sub_keys, peer_u, peer_v, final_norm_g):
    bsz, s, d = x.shape
    depth = w_ada.shape[0]
    assert bsz == 1 and depth == 1, "one sequence, one layer"
    cw = conv_w.shape[-1]
    aw = N_ATTN_HEADS * HEAD_DIM
    iw = N_IDX_HEADS * IDX_HEAD_DIM
    assert cw == aw
    x2d = x.reshape(s, d)
    pos = positions.reshape(s)

    w_in2 = w_in.reshape(d, -1)
    o_q, o_k, o_v, o_qi = 2 * cw, 2 * cw + aw, 2 * cw + 2 * aw, 2 * cw + 3 * aw
    o_ki = o_qi + iw
    o_wi = o_ki + IDX_HEAD_DIM
    o_g = o_wi + N_IDX_HEADS
    w_glu = w_in2[:, :o_q].astype(bf16)
    w_feat_t = jnp.concatenate([w_in2[:, o_qi:o_ki], w_in2[:, o_q:o_k], w_in2[:, o_v:o_qi]], axis=1).T.astype(bf16)
    w_k = w_in2[:, o_k:o_v].astype(bf16)
    w_kw = w_in2[:, o_ki:o_ki + 2 * LANE].astype(bf16)
    w_gate = w_in2[:, o_g:].astype(bf16)
    w_out2 = w_out.reshape(cw + aw, d).astype(bf16)
    wq_t = peer_w_q.reshape(d, -1).T.astype(bf16)
    sub_keys = peer_sub_keys.reshape(peer_sub_keys.shape[1:]).astype(bf16)
    u_exp = peer_u.reshape(-1, d).astype(bf16)
    v_exp = peer_v.reshape(-1, d).astype(bf16)

    cos_f, sin_f = _rope_tables(pos, HEAD_DIM)
    cos_p, sin_p = _rope_tables(pos, IDX_ROPE_DIM)
    cos_tok = jnp.concatenate([cos_f, cos_f], axis=1)
    sin_tok = jnp.concatenate([-sin_f, sin_f], axis=1)
    zeros_p = jnp.zeros_like(sin_p)
    pad = jnp.zeros((s, LANE - IDX_ROPE_DIM), f32)
    c_tok = jnp.concatenate([cos_p, cos_p, pad + 1.0], axis=1)
    s1_tok = jnp.concatenate([zeros_p, sin_p, pad], axis=1)
    s2_tok = jnp.concatenate([-sin_p, zeros_p, pad], axis=1)

    ada = _ada(jnp.broadcast_to(c.reshape(1, d), (SUBLANE, d)), w_ada.reshape(d, -1), b_ada.reshape(1, -1))[0:1]
    shift1, scale1, gate1, shift2, scale2, gate2 = jnp.split(ada, 6, axis=-1)

    h = _norm_mod(x2d, norm_mix_g.reshape(1, d), scale1, shift1)
    u = _glu(h, w_glu, cw)
    a_out = _conv(u, conv_w.reshape(-1, cw), conv_b.reshape(1, cw), conv_ln_g.reshape(1, cw), conv_ln_b.reshape(1, cw))
    feat_t = _proj_t(w_feat_t, h, cos_f.T, sin_f.T, cos_p.T, sin_p.T, iw, aw)
    k_tok = _proj_k(h, w_k, cos_tok, sin_tok)
    w_scale = float(N_IDX_HEADS ** -0.5 * IDX_HEAD_DIM ** -0.5)
    kidx, widx_t = _kw(h, w_kw, k_idx_ln_g.reshape(1, -1), k_idx_ln_b.reshape(1, -1), c_tok, s1_tok, s2_tok, w_scale)
    b_out = _dsa(feat_t, k_tok, kidx, widx_t, iw, aw, min(TOPK_MAX, s // 4))
    x1 = _merge(h, a_out, b_out, w_gate, w_out2, x2d, gate1)

    h2 = _norm_mod(x1, norm_ffn_g.reshape(1, d), scale2, shift2)
    a, b, ea, eb, thr = _route(h2, wq_t, sub_keys)
    p = _peer(h2, u_exp, v_exp, a, b, ea, eb, thr)
    return _final(x1, p, gate2, final_norm_g.reshape(1, d)).reshape(bsz, s, d)
```

```python
import functools

import numpy as np
import jax
import jax.numpy as jnp
from jax import lax
from jax.experimental import pallas as pl
from jax.experimental.pallas import tpu as pltpu

N_ATTN_HEADS = 16
HEAD_DIM = 128
N_IDX_HEADS = 32
IDX_HEAD_DIM = 128
IDX_ROPE_DIM = 64
TOPK_MAX = 256
ROPE_THETA = 10000.0
PEER_TOPK = 16
EPS = 1e-6

LANE = 128
SUBLANE = 8
VMEM_LIMIT = 56 * 1024 * 1024
NEG_BIG = -1e30
LOG2E = 1.4426950408889634
INT_MIN = -2 ** 31
INT_MAX = 2 ** 31 - 1

TM_NORM = 512
TM_MM = 1024
TN_MM = 512
TM_CONV = 256
TQ_DSA = 256
TK_DSA = 1024
TM_MERGE = 512
TN_MERGE = 256
TT_ROUTE = 512
TT_PEER = 512
EC_PEER = 512
SUB_PEER = 256
TM_FINAL = 256

f32 = jnp.float32
bf16 = jnp.bfloat16
i32 = jnp.int32

_NT = (((1,), (1,)), ((), ()))
_TN = (((0,), (0,)), ((), ()))


def _cp(sem, vmem=VMEM_LIMIT, flags=None):
    return pltpu.CompilerParams(dimension_semantics=sem, vmem_limit_bytes=vmem, flags=flags)


_INTERLEAVE = None


def _tile(n, t):
    t = min(n, t)
    assert n % t == 0, (n, t)
    return t


def _ada_kernel(c_ref, w_ref, b_ref, o_ref):
    c = c_ref[...]
    ca = (c * jax.nn.sigmoid(c)).astype(bf16)
    o_ref[...] = jnp.dot(ca, w_ref[...].astype(bf16), preferred_element_type=f32) + b_ref[...]


def _ada(c8, w, b):
    d, n = w.shape
    tn = _tile(n, 512)
    return pl.pallas_call(
        _ada_kernel,
        grid=(n // tn,),
        in_specs=[pl.BlockSpec((SUBLANE, d), lambda j: (0, 0)),
                  pl.BlockSpec((d, tn), lambda j: (0, j)),
                  pl.BlockSpec((1, tn), lambda j: (0, j))],
        out_specs=pl.BlockSpec((SUBLANE, tn), lambda j: (0, j)),
        out_shape=jax.ShapeDtypeStruct((SUBLANE, n), f32),
        compiler_params=_cp(("arbitrary",)),
        name="ada",
    )(c8, w, b)


def _norm_mod_kernel(x_ref, g_ref, sc_ref, sh_ref, o_ref):
    x = x_ref[...]
    y = x * lax.rsqrt(jnp.mean(x * x, axis=-1, keepdims=True) + EPS) * g_ref[...]
    o_ref[...] = (y * (1.0 + sc_ref[...]) + sh_ref[...]).astype(o_ref.dtype)


def _norm_mod_t_kernel(x_ref, g_ref, sc_ref, sh_ref, o_ref):
    x = x_ref[...]
    y = x * lax.rsqrt(jnp.mean(x * x, axis=-1, keepdims=True) + EPS) * g_ref[...]
    o_ref[...] = (y * (1.0 + sc_ref[...]) + sh_ref[...]).T.astype(o_ref.dtype)


def _norm_mod(x, g, scale, shift, transposed=False):
    s, d = x.shape
    tm = _tile(s, TM_NORM)
    row = pl.BlockSpec((1, d), lambda i: (0, 0))
    return pl.pallas_call(
        _norm_mod_t_kernel if transposed else _norm_mod_kernel,
        grid=(s // tm,),
        in_specs=[pl.BlockSpec((tm, d), lambda i: (i, 0)), row, row, row],
        out_specs=pl.BlockSpec((d, tm), lambda i: (0, i)) if transposed else pl.BlockSpec((tm, d), lambda i: (i, 0)),
        out_shape=jax.ShapeDtypeStruct((d, s) if transposed else (s, d), bf16),
        compiler_params=_cp(("arbitrary",)),
        name="norm_mod_t" if transposed else "norm_mod",
    )(x, g, scale, shift)


def _glu_kernel(h_ref, wa_ref, wg_ref, o_ref):
    h = h_ref[...]
    a = jnp.dot(h, wa_ref[...], preferred_element_type=f32)
    g = jnp.dot(h, wg_ref[...], preferred_element_type=f32)
    o_ref[...] = a * jax.nn.sigmoid(g)


def _glu(h, w_glu, cw):
    s, d = h.shape
    tm, tn = _tile(s, TM_MM), _tile(cw, TN_MM)
    nj = cw // tn
    return pl.pallas_call(
        _glu_kernel,
        grid=(s // tm, nj),
        in_specs=[pl.BlockSpec((tm, d), lambda i, j: (i, 0)),
                  pl.BlockSpec((d, tn), lambda i, j: (0, j)),
                  pl.BlockSpec((d, tn), lambda i, j: (0, j + nj))],
        out_specs=pl.BlockSpec((tm, tn), lambda i, j: (i, j)),
        out_shape=jax.ShapeDtypeStruct((s, cw), f32),
        compiler_params=_cp(("arbitrary", "arbitrary")),
        name="glu",
    )(h, w_glu, w_glu)


def _conv_kernel(prev_ref, cur_ref, w_ref, cb_ref, g_ref, b_ref, o_ref, ubuf, ybuf, *, tm, kw, halo, rs):
    i = pl.program_id(0)
    cw = cur_ref.shape[1]
    ubuf[0:halo, :] = jnp.where(i > 0, prev_ref[...], 0.0)
    ubuf[halo:, :] = cur_ref[...]
    off = halo - (kw - 1)

    def row_body(r, carry):
        r0 = pl.multiple_of(r * rs, rs)
        for c in range(cw // LANE):
            cs = slice(c * LANE, (c + 1) * LANE)
            acc = jnp.zeros((rs, LANE), f32) + cb_ref[:, cs]
            win = ubuf[pl.ds(r0, rs + halo), cs]
            for k in range(kw):
                acc = acc + w_ref[k:k + 1, cs] * win[off + k:off + k + rs]
            ybuf[pl.ds(r0, rs), cs] = acc
        return carry

    lax.fori_loop(0, tm // rs, row_body, 0)
    y = ybuf[...]
    mu = jnp.mean(y, axis=-1, keepdims=True)
    yc = y - mu
    var = jnp.mean(yc * yc, axis=-1, keepdims=True)
    z = yc * lax.rsqrt(var + EPS) * g_ref[...] + b_ref[...]
    o_ref[...] = (z * jax.nn.sigmoid(z)).astype(o_ref.dtype)


def _conv(u, conv_w, conv_b, ln_g, ln_b):
    s, cw = u.shape
    kw = conv_w.shape[0]
    tm = _tile(s, TM_CONV)
    halo = -(-(kw - 1) // SUBLANE) * SUBLANE
    rs = 32
    assert tm % halo == 0 and tm % rs == 0
    kwp = -(-kw // SUBLANE) * SUBLANE
    w_pad = jnp.zeros((kwp, cw), f32).at[:kw].set(conv_w)
    row = pl.BlockSpec((1, cw), lambda i: (0, 0))
    return pl.pallas_call(
        functools.partial(_conv_kernel, tm=tm, kw=kw, halo=halo, rs=rs),
        grid=(s // tm,),
        in_specs=[pl.BlockSpec((halo, cw), lambda i: (jnp.maximum(i * (tm // halo) - 1, 0), 0)),
                  pl.BlockSpec((tm, cw), lambda i: (i, 0)),
                  pl.BlockSpec((kwp, cw), lambda i: (0, 0)), row, row, row],
        out_specs=pl.BlockSpec((tm, cw), lambda i: (i, 0)),
        out_shape=jax.ShapeDtypeStruct((s, cw), bf16),
        scratch_shapes=[pltpu.VMEM((halo + tm, cw), f32), pltpu.VMEM((tm, cw), f32)],
        compiler_params=_cp(("arbitrary",)),
        name="conv",
    )(u, u, w_pad, conv_b, ln_g, ln_b)


def _proj_t_kernel(w_ref, h_ref, cf_ref, sf_ref, cp_ref, sp_ref, o_ref, *, n_idx_tiles, n_rope_tiles, tn, q_mult):
    j = pl.program_id(1)
    acc = lax.dot_general(w_ref[...], h_ref[...], _NT, preferred_element_type=f32)

    def rope_rows(cos_ref, sin_ref, rot, mult=None):
        half = rot // 2
        cos, sin = cos_ref[...], sin_ref[...]
        if mult is not None:
            cos, sin = cos * mult, sin * mult
        for hd in range(tn // LANE):
            r0 = hd * LANE
            x1 = acc[r0:r0 + half]
            x2 = acc[r0 + half:r0 + rot]
            o_ref[r0:r0 + half, :] = (x1 * cos - x2 * sin).astype(o_ref.dtype)
            o_ref[r0 + half:r0 + rot, :] = (x2 * cos + x1 * sin).astype(o_ref.dtype)
            if rot < LANE:
                o_ref[r0 + rot:r0 + LANE, :] = acc[r0 + rot:r0 + LANE].astype(o_ref.dtype)

    @pl.when(j < n_idx_tiles)
    def _():
        rope_rows(cp_ref, sp_ref, IDX_ROPE_DIM)

    @pl.when((j >= n_idx_tiles) & (j < n_idx_tiles + n_rope_tiles))
    def _():
        rope_rows(cf_ref, sf_ref, HEAD_DIM, q_mult)

    @pl.when(j >= n_idx_tiles + n_rope_tiles)
    def _():
        o_ref[...] = acc.astype(o_ref.dtype)


def _proj_t(wt, h, cos_f, sin_f, cos_p, sin_p, n_idx_rows, n_rope_rows, q_mult):
    n, d = wt.shape
    s = h.shape[0]
    tm, tn = _tile(s, TM_MM), _tile(n_rope_rows, TN_MM)
    assert n % tn == 0 and n_idx_rows % tn == 0 and n_rope_rows % tn == 0
    kern = functools.partial(_proj_t_kernel, n_idx_tiles=n_idx_rows // tn, n_rope_tiles=n_rope_rows // tn, tn=tn,
                             q_mult=q_mult)
    tab = lambda t: pl.BlockSpec((t.shape[0], tm), lambda i, j: (0, i))
    return pl.pallas_call(
        kern,
        grid=(s // tm, n // tn),
        in_specs=[pl.BlockSpec((tn, d), lambda i, j: (j, 0)),
                  pl.BlockSpec((tm, d), lambda i, j: (i, 0)),
                  tab(cos_f), tab(sin_f), tab(cos_p), tab(sin_p)],
        out_specs=pl.BlockSpec((tn, tm), lambda i, j: (j, i)),
        out_shape=jax.ShapeDtypeStruct((n, s), bf16),
        compiler_params=_cp(("arbitrary", "arbitrary")),
        name="proj_t",
    )(wt, h, cos_f, sin_f, cos_p, sin_p)


def _proj_k_kernel(h_ref, w_ref, c_ref, s_ref, o_ref, *, tn):
    acc = jnp.dot(h_ref[...], w_ref[...], preferred_element_type=f32)
    cos, sin = c_ref[...], s_ref[...]
    for hd in range(tn // LANE):
        x = acc[:, hd * LANE:(hd + 1) * LANE]
        y = x * cos + pltpu.roll(x, HEAD_DIM // 2, 1) * sin
        o_ref[:, hd * LANE:(hd + 1) * LANE] = y.astype(o_ref.dtype)


def _proj_k(h, w, cos_tok, sin_tok):
    s, d = h.shape
    n = w.shape[1]
    tm, tn = _tile(s, TM_MM), _tile(n, TN_MM)
    tab = pl.BlockSpec((tm, LANE), lambda i, j: (i, 0))
    return pl.pallas_call(
        functools.partial(_proj_k_kernel, tn=tn),
        grid=(s // tm, n // tn),
        in_specs=[pl.BlockSpec((tm, d), lambda i, j: (i, 0)),
                  pl.BlockSpec((d, tn), lambda i, j: (0, j)), tab, tab],
        out_specs=pl.BlockSpec((tm, tn), lambda i, j: (i, j)),
        out_shape=jax.ShapeDtypeStruct((s, n), bf16),
        compiler_params=_cp(("arbitrary", "arbitrary")),
        name="proj_k",
    )(h, w, cos_tok, sin_tok)


def _kw_kernel(h_ref, w_ref, g_ref, b_ref, c_ref, s1_ref, s2_ref, kidx_ref, widx_ref, *, w_scale):
    acc = jnp.dot(h_ref[...], w_ref[...], preferred_element_type=f32)
    k = acc[:, :LANE]
    mu = jnp.mean(k, axis=-1, keepdims=True)
    kc = k - mu
    var = jnp.mean(kc * kc, axis=-1, keepdims=True)
    y = kc * lax.rsqrt(var + EPS) * g_ref[...] + b_ref[...]
    half = IDX_ROPE_DIM // 2
    y = y * c_ref[...] + pltpu.roll(y, half, 1) * s1_ref[...] + pltpu.roll(y, LANE - half, 1) * s2_ref[...]
    kidx_ref[...] = y.astype(kidx_ref.dtype)
    widx_ref[...] = (acc[:, LANE:] * w_scale).T


def _kw(h, w_kw, ln_g, ln_b, c_tok, s1_tok, s2_tok, w_scale):
    s, d = h.shape
    tm = _tile(s, TM_NORM)
    row = pl.BlockSpec((1, LANE), lambda i: (0, 0))
    tab = pl.BlockSpec((tm, LANE), lambda i: (i, 0))
    return pl.pallas_call(
        functools.partial(_kw_kernel, w_scale=w_scale),
        grid=(s // tm,),
        in_specs=[pl.BlockSpec((tm, d), lambda i: (i, 0)),
                  pl.BlockSpec((d, 2 * LANE), lambda i: (0, 0)), row, row, tab, tab, tab],
        out_specs=[pl.BlockSpec((tm, LANE), lambda i: (i, 0)),
                   pl.BlockSpec((LANE, tm), lambda i: (0, i))],
        out_shape=[jax.ShapeDtypeStruct((s, LANE), bf16), jax.ShapeDtypeStruct((LANE, s), f32)],
        compiler_params=_cp(("arbitrary",)),
        name="kw",
    )(h, w_kw, ln_g, ln_b, c_tok, s1_tok, s2_tok)


def _dsa_kernel(qidx_ref, q_ref, k_ref, v_ref, kidx_ref, w_ref, o_ref,
                key_ref, thr_ref, m_ref, l_ref, acc_ref, *, tq, tk, n_heads, n_idx_heads, k_sel, n_split):
    i = pl.program_id(0)
    j = pl.program_id(1)
    last = ((i + 1) * tq - 1) // tk
    nkb = last + 1
    neg_key = int(np.array(-np.inf, np.float32).view(np.int32)) ^ 0x7FFFFFFF

    @pl.when(j == 0)
    def _score_and_select():
        m_ref[...] = jnp.full(m_ref.shape, NEG_BIG, f32)
        l_ref[...] = jnp.zeros(l_ref.shape, f32)
        acc_ref[...] = jnp.zeros(acc_ref.shape, f32)
        qpos = i * tq + lax.broadcasted_iota(i32, (tk, tq), 1)

        def score_body(c, carry):
            kc = kidx_ref[pl.ds(pl.multiple_of(c * tk, tk), tk), :]
            acc = jnp.zeros((tk, tq), f32)
            for h in range(n_idx_heads):
                lg = jnp.dot(kc, qidx_ref[h * LANE:(h + 1) * LANE, :], preferred_element_type=f32)
                acc = acc + w_ref[h:h + 1, :] * jnp.maximum(lg, 0.0)
            kpos = c * tk + lax.broadcasted_iota(i32, (tk, tq), 0)
            acc = jnp.where(kpos <= qpos, acc, -jnp.inf)
            bits = pltpu.bitcast(acc, i32)
            key_ref[c] = bits ^ ((bits >> 31) & 0x7FFFFFFF)
            return carry

        lax.fori_loop(0, nkb, score_body, 0)

        def bis_body(it, carry):
            lo, hi = carry
            mid = (lo >> 1) + (hi >> 1) + (lo & hi & 1)

            def cnt_body(c, cnt):
                parts = [cnt, jnp.zeros_like(cnt), jnp.zeros_like(cnt), jnp.zeros_like(cnt)]
                for g in range(tk // SUBLANE):
                    blk = key_ref[c, g * SUBLANE:(g + 1) * SUBLANE, :]
                    parts[g % 4] = parts[g % 4] + jnp.where(blk >= mid, 1.0, 0.0)
                return (parts[0] + parts[1]) + (parts[2] + parts[3])

            cnt = lax.fori_loop(0, nkb, cnt_body, jnp.zeros((SUBLANE, tq), f32))
            tot = jnp.broadcast_to(jnp.sum(cnt, axis=0, keepdims=True), (SUBLANE, tq))
            ge = tot >= float(k_sel)
            return jnp.where(ge, mid, lo), jnp.where(ge, hi, mid)

        lo0 = jnp.full((SUBLANE, tq), INT_MIN, i32)
        hi0 = jnp.full((SUBLANE, tq), INT_MAX, i32)
        lo, _ = lax.fori_loop(0, 32, bis_body, (lo0, hi0))
        thr_ref[...] = jnp.maximum(lo, neg_key + 1)

    @pl.when(j <= last)
    def _attend():
        parts = [slice(r * (tk // n_split), (r + 1) * (tk // n_split)) for r in range(n_split)]
        bias = [jnp.where(key_ref[j, ks, :] >= thr_ref[0:1, :], 0.0, NEG_BIG) for ks in parts]
        for h in range(n_heads):
            hs = slice(h * HEAD_DIM, (h + 1) * HEAD_DIM)
            q = q_ref[hs, :]
            sts = [jnp.dot(k_ref[ks, hs], q, preferred_element_type=f32) + bs for ks, bs in zip(parts, bias)]
            m_blk = functools.reduce(jnp.maximum, [jnp.max(st, axis=0, keepdims=True) for st in sts])
            m_old = m_ref[h]
            m_new = jnp.maximum(m_old, m_blk)
            alpha = jnp.exp2(m_old - m_new)
            ps = [jnp.exp2(st - m_new[0:1, :]) for st in sts]
            l_ref[h] = alpha * l_ref[h] + sum(jnp.sum(p, axis=0, keepdims=True) for p in ps)
            pv = sum(jnp.dot(v_ref[hs, ks], p.astype(bf16), preferred_element_type=f32)
                     for ks, p in zip(parts, ps))
            acc_ref[hs, :] = alpha[0:1, :] * acc_ref[hs, :] + pv
            m_ref[h] = m_new

    @pl.when(j == last)
    def _finish():
        for h in range(n_heads):
            hs = slice(h * HEAD_DIM, (h + 1) * HEAD_DIM)
            out_t = acc_ref[hs, :] / l_ref[h][0:1, :]
            o_ref[:, hs] = out_t.T.astype(o_ref.dtype)


def _dsa(feat_t, k_tok, kidx, widx_t, n_idx_rows, attn_w, k_sel):
    s = k_tok.shape[0]
    tq, tk = _tile(s, TQ_DSA), _tile(s, TK_DSA)
    assert tk >= k_sel and n_idx_rows % attn_w == 0 and tk % tq == 0
    nq, nk = s // tq, s // tk
    qb = n_idx_rows // attn_w
    last = lambda i: ((i + 1) * tq - 1) // tk
    kern = functools.partial(_dsa_kernel, tq=tq, tk=tk, n_heads=attn_w // HEAD_DIM,
                             n_idx_heads=n_idx_rows // IDX_HEAD_DIM, k_sel=k_sel, n_split=8)
    return pl.pallas_call(
        kern,
        grid=(nq, nk),
        in_specs=[pl.BlockSpec((n_idx_rows, tq), lambda i, j: (0, i)),
                  pl.BlockSpec((attn_w, tq), lambda i, j: (qb, i)),
                  pl.BlockSpec((tk, attn_w), lambda i, j: (jnp.minimum(j, last(i)), 0)),
                  pl.BlockSpec((attn_w, tk), lambda i, j: (qb + 1, jnp.minimum(j, last(i)))),
                  pl.BlockSpec((s, LANE), lambda i, j: (0, 0)),
                  pl.BlockSpec((LANE, tq), lambda i, j: (0, i))],
        out_specs=pl.BlockSpec((tq, attn_w), lambda i, j: (i, 0)),
        out_shape=jax.ShapeDtypeStruct((s, attn_w), bf16),
        scratch_shapes=[pltpu.VMEM((nk, tk, tq), i32),
                        pltpu.VMEM((SUBLANE, tq), i32),
                        pltpu.VMEM((attn_w // HEAD_DIM, SUBLANE, tq), f32),
                        pltpu.VMEM((attn_w // HEAD_DIM, SUBLANE, tq), f32),
                        pltpu.VMEM((attn_w, tq), f32)],
        compiler_params=_cp(("arbitrary", "arbitrary")),
        name="dsa",
    )(feat_t, feat_t, k_tok, feat_t, kidx, widx_t)


def _merge_kernel(h_ref, a_ref, b_ref, wg0_ref, wg1_ref, wo0_ref, wo1_ref, x_ref, g_ref, o_ref):
    h = h_ref[...]
    g0 = jax.nn.sigmoid(jnp.dot(h, wg0_ref[...], preferred_element_type=f32))
    g1 = jax.nn.sigmoid(jnp.dot(h, wg1_ref[...], preferred_element_type=f32))
    ya = jnp.dot(a_ref[...], wo0_ref[...], preferred_element_type=f32)
    yb = jnp.dot(b_ref[...], wo1_ref[...], preferred_element_type=f32)
    o_ref[...] = x_ref[...] + g_ref[...] * (g0 * ya + g1 * yb)


def _merge(h, a_out, b_out, w_gate, w_out, x, gate1):
    s, d = x.shape
    cw = a_out.shape[1]
    assert b_out.shape[1] == cw
    tm, tn = _tile(s, TM_MERGE), _tile(d, TN_MERGE)
    nj = d // tn
    return pl.pallas_call(
        _merge_kernel,
        grid=(s // tm, nj),
        in_specs=[pl.BlockSpec((tm, d), lambda i, j: (i, 0)),
                  pl.BlockSpec((tm, cw), lambda i, j: (i, 0)),
                  pl.BlockSpec((tm, cw), lambda i, j: (i, 0)),
                  pl.BlockSpec((d, tn), lambda i, j: (0, j)),
                  pl.BlockSpec((d, tn), lambda i, j: (0, j + nj)),
                  pl.BlockSpec((cw, tn), lambda i, j: (0, j)),
                  pl.BlockSpec((cw, tn), lambda i, j: (1, j)),
                  pl.BlockSpec((tm, tn), lambda i, j: (i, j)),
                  pl.BlockSpec((1, tn), lambda i, j: (0, j))],
        out_specs=pl.BlockSpec((tm, tn), lambda i, j: (i, j)),
        out_shape=jax.ShapeDtypeStruct((s, d), f32),
        compiler_params=_cp(("arbitrary", "arbitrary")),
        name="merge",
    )(h, a_out, b_out, w_gate, w_gate, w_out, w_out, x, gate1)


def _top_sorted(x, k):
    n = x.shape[0]
    rows = lax.broadcasted_iota(i32, x.shape, 0).astype(f32)
    cur, tops = x, []
    for _ in range(k):
        m = jnp.max(cur, axis=0, keepdims=True)
        first = jnp.min(jnp.where(cur == m, rows, float(n)), axis=0, keepdims=True)
        cur = jnp.where(rows == first, -jnp.inf, cur)
        tops.append(m)
    return tops


def _route_kernel(h_ref, wq_ref, sk_ref, a_ref, b_ref, ea_ref, eb_ref, thr_ref, *, half, topk):
    qt = jnp.dot(wq_ref[...], h_ref[...], preferred_element_type=f32).astype(bf16)
    a = jnp.dot(sk_ref[0, 0], qt[:half], preferred_element_type=f32)
    b = jnp.dot(sk_ref[0, 1], qt[half:], preferred_element_type=f32)
    ta = _top_sorted(a, topk)
    tb = jnp.concatenate(_top_sorted(b, topk), axis=0)
    n_q = lambda p: -(-(topk // (p + 1)) // SUBLANE) * SUBLANE
    cand = jnp.concatenate([t + tb[:n_q(p)] for p, t in enumerate(ta)], axis=0)
    best = _top_sorted(cand, topk)
    thr = best[-1]
    z = jnp.ones_like(thr)
    for t in best[1:]:
        z = z + jnp.exp(t - best[0])
    a_ref[0] = a
    b_ref[0] = b
    ea_ref[0] = jnp.exp(a - ta[0])
    eb_ref[0] = jnp.exp(b - tb[0:1]) / z
    thr_ref[0] = jnp.broadcast_to(thr, thr_ref.shape[1:])


def _route(h2t, wq_t, sub_keys):
    d, s = h2t.shape
    heads, _, n_keys, half = sub_keys.shape
    tt = _tile(s, TT_ROUTE)
    big = pl.BlockSpec((1, n_keys, tt), lambda t, h: (h, 0, t))
    shp = jax.ShapeDtypeStruct((heads, n_keys, s), f32)
    return pl.pallas_call(
        functools.partial(_route_kernel, half=half, topk=PEER_TOPK),
        grid=(s // tt, heads),
        in_specs=[pl.BlockSpec((d, tt), lambda t, h: (0, t)),
                  pl.BlockSpec((2 * half, d), lambda t, h: (h, 0)),
                  pl.BlockSpec((1, 2, n_keys, half), lambda t, h: (h, 0, 0, 0))],
        out_specs=[big, big, big, big, pl.BlockSpec((1, SUBLANE, tt), lambda t, h: (h, 0, t))],
        out_shape=[shp, shp, shp, shp, jax.ShapeDtypeStruct((heads, SUBLANE, s), f32)],
        compiler_params=_cp(("arbitrary", "arbitrary")),
        name="route",
    )(h2t, wq_t, sub_keys)


def _peer_kernel(h_ref, u_ref, v_ref, a_ref, b_ref, ea_ref, eb_ref, thr_ref, o_ref, at_ref, g_ref, wt_ref,
                 *, ec, sub, heads, n_keys, lw):
    e = pl.program_id(1)
    d, tt = h_ref.shape
    nsub = ec // sub
    pieces = [(ii, lh) for ii in range(sub // n_keys) for lh in range(tt // lw)]
    dn = d // len(pieces)

    @pl.when(e == 0)
    def _():
        o_ref[...] = jnp.zeros(o_ref.shape, f32)

    def gate_piece(c, ii, lh):
        row = e * (ec // n_keys) + c * (sub // n_keys) + ii
        rs = slice(c * sub + ii * n_keys, c * sub + (ii + 1) * n_keys)
        ls = slice(lh * lw, (lh + 1) * lw)
        g = jnp.zeros((n_keys, lw), f32)
        for hd in range(heads):
            val = a_ref[hd, pl.ds(row, 1), :][:, ls] + b_ref[hd, :, ls]
            g = g + jnp.where(val >= thr_ref[hd, 0:1, ls],
                              ea_ref[hd, pl.ds(row, 1), :][:, ls] * eb_ref[hd, :, ls], 0.0)
        g_ref[rs, ls] = g

    def combine(c):
        for ii, lh in pieces:
            rs = slice(c * sub + ii * n_keys, c * sub + (ii + 1) * n_keys)
            ls = slice(lh * lw, (lh + 1) * lw)
            at = at_ref[rs, ls]
            act = 0.5 * at * (1.0 + lax.erf(at * np.float32(np.sqrt(0.5))))
            wt_ref[ls, rs] = (g_ref[rs, ls] * act).T.astype(bf16)

    def out_chunk(c, k):
        cs = slice(c * sub, (c + 1) * sub)
        ns = slice(k * dn, (k + 1) * dn)
        o_ref[:, ns] += jnp.dot(wt_ref[:, cs], v_ref[cs, ns], preferred_element_type=f32)

    h = h_ref[...]
    for p in pieces:
        gate_piece(0, *p)
    for c in range(nsub):
        cs = slice(c * sub, (c + 1) * sub)
        at_ref[cs, :] = jnp.dot(u_ref[cs, :], h, preferred_element_type=f32)
    for c in range(nsub):
        combine(c)
        for k, p in enumerate(pieces):
            out_chunk(c, k)
            if c + 1 < nsub:
                gate_piece(c + 1, *p)


def _peer(h2t, u, v, a, b, ea, eb, thr):
    d, s = h2t.shape
    n_exp = u.shape[0]
    heads, n_keys, _ = a.shape
    tt, ec = _tile(s, TT_PEER), _tile(n_exp, EC_PEER)
    sub = min(ec, SUB_PEER)
    assert ec % sub == 0 and sub % n_keys == 0
    once = pl.Buffered(1)
    tab = pl.BlockSpec((heads, n_keys, tt), lambda t, e: (0, 0, t), pipeline_mode=once)
    return pl.pallas_call(
        functools.partial(_peer_kernel, ec=ec, sub=sub, heads=heads, n_keys=n_keys, lw=LANE),
        grid=(s // tt, n_exp // ec),
        in_specs=[pl.BlockSpec((d, tt), lambda t, e: (0, t), pipeline_mode=once),
                  pl.BlockSpec((ec, d), lambda t, e: (e, 0)),
                  pl.BlockSpec((ec, d), lambda t, e: (e, 0)),
                  tab, tab, tab, tab,
                  pl.BlockSpec((heads, SUBLANE, tt), lambda t, e: (0, 0, t), pipeline_mode=once)],
        out_specs=pl.BlockSpec((tt, d), lambda t, e: (t, 0)),
        out_shape=jax.ShapeDtypeStruct((s, d), f32),
        scratch_shapes=[pltpu.VMEM((ec, tt), f32), pltpu.VMEM((ec, tt), f32), pltpu.VMEM((tt, ec), bf16)],
        compiler_params=_cp(("arbitrary", "arbitrary"), flags=_INTERLEAVE),
        name="peer",
    )(h2t, u, v, a, b, ea, eb, thr)


def _final_kernel(x_ref, p_ref, g2_ref, fg_ref, o_ref):
    x = x_ref[...] + g2_ref[...] * p_ref[...]
    o_ref[...] = x * lax.rsqrt(jnp.mean(x * x, axis=-1, keepdims=True) + EPS) * fg_ref[...]


def _final(x1, p, gate2, fg):
    s, d = x1.shape
    tm = _tile(s, TM_FINAL)
    row = pl.BlockSpec((1, d), lambda i: (0, 0))
    blk = pl.BlockSpec((tm, d), lambda i: (i, 0))
    return pl.pallas_call(
        _final_kernel,
        grid=(s // tm,),
        in_specs=[blk, blk, row, row],
        out_specs=blk,
        out_shape=jax.ShapeDtypeStruct((s, d), f32),
        compiler_params=_cp(("arbitrary",)),
        name="final",
    )(x1, p, gate2, fg)


def _rope_tables(positions, dim):
    inv_freq = ROPE_THETA ** (-jnp.arange(0, dim, 2, dtype=f32) / dim)
    ang = positions.astype(f32)[:, None] * inv_freq
    return jnp.cos(ang), jnp.sin(ang)


def kernel(x, c, positions, w_ada, b_ada, norm_mix_g, norm_ffn_g, w_in, conv_w, conv_b, conv_ln_g, conv_ln_b,
           k_idx_ln_g, k_idx_ln_b, w_out, peer_w_q, peer_sub_keys, peer_u, peer_v, final_norm_g):
    bsz, s, d = x.shape
    depth = w_ada.shape[0]
    assert bsz == 1 and depth == 1, "one sequence, one layer"
    cw = conv_w.shape[-1]
    aw = N_ATTN_HEADS * HEAD_DIM
    iw = N_IDX_HEADS * IDX_HEAD_DIM
    assert cw == aw
    x2d = x.reshape(s, d)
    pos = positions.reshape(s)

    w_in2 = w_in.reshape(d, -1)
    o_q, o_k, o_v, o_qi = 2 * cw, 2 * cw + aw, 2 * cw + 2 * aw, 2 * cw + 3 * aw
    o_ki = o_qi + iw
    o_wi = o_ki + IDX_HEAD_DIM
    o_g = o_wi + N_IDX_HEADS
    w_glu = w_in2[:, :o_q].astype(bf16)
    w_feat_t = jnp.concatenate([w_in2[:, o_qi:o_ki], w_in2[:, o_q:o_k], w_in2[:, o_v:o_qi]], axis=1).T.astype(bf16)
    w_k = w_in2[:, o_k:o_v].astype(bf16)
    w_kw = w_in2[:, o_ki:o_ki + 2 * LANE].astype(bf16)
    w_gate = w_in2[:, o_g:].astype(bf16)
    w_out2 = w_out.reshape(cw + aw, d).astype(bf16)
    wq_t = peer_w_q.reshape(d, -1).T.astype(bf16)
    sub_keys = peer_sub_keys.reshape(peer_sub_keys.shape[1:]).astype(bf16)
    u_exp = peer_u.reshape(-1, d).astype(bf16)
    v_exp = peer_v.reshape(-1, d).astype(bf16)

    cos_f, sin_f = _rope_tables(pos, HEAD_DIM)
    cos_p, sin_p = _rope_tables(pos, IDX_ROPE_DIM)
    cos_tok = jnp.concatenate([cos_f, cos_f], axis=1)
    sin_tok = jnp.concatenate([-sin_f, sin_f], axis=1)
    zeros_p = jnp.zeros_like(sin_p)
    pad = jnp.zeros((s, LANE - IDX_ROPE_DIM), f32)
    c_tok = jnp.concatenate([cos_p, cos_p, pad + 1.0], axis=1)
    s1_tok = jnp.concatenate([zeros_p, sin_p, pad], axis=1)
    s2_tok = jnp.concatenate([-sin_p, zeros_p, pad], axis=1)

    ada = _ada(jnp.broadcast_to(c.reshape(1, d), (SUBLANE, d)), w_ada.reshape(d, -1), b_ada.reshape(1, -1))[0:1]
    shift1, scale1, gate1, shift2, scale2, gate2 = jnp.split(ada, 6, axis=-1)

    h = _norm_mod(x2d, norm_mix_g.reshape(1, d), scale1, shift1)
    u = _glu(h, w_glu, cw)
    a_out = _conv(u, conv_w.reshape(-1, cw), conv_b.reshape(1, cw), conv_ln_g.reshape(1, cw), conv_ln_b.reshape(1, cw))
    feat_t = _proj_t(w_feat_t, h, cos_f.T, sin_f.T, cos_p.T, sin_p.T, iw, aw, HEAD_DIM ** -0.5 * LOG2E)
    k_tok = _proj_k(h, w_k, cos_tok, sin_tok)
    w_scale = float(N_IDX_HEADS ** -0.5 * IDX_HEAD_DIM ** -0.5)
    kidx, widx_t = _kw(h, w_kw, k_idx_ln_g.reshape(1, -1), k_idx_ln_b.reshape(1, -1), c_tok, s1_tok, s2_tok, w_scale)
    b_out = _dsa(feat_t, k_tok, kidx, widx_t, iw, aw, min(TOPK_MAX, s // 4))
    x1 = _merge(h, a_out, b_out, w_gate, w_out2, x2d, gate1)

    h2t = _norm_mod(x1, norm_ffn_g.reshape(1, d), scale2, shift2, transposed=True)
    a, b, ea, eb, thr = _route(h2t, wq_t, sub_keys)
    p = _peer(h2t, u_exp, v_exp, a, b, ea, eb, thr)
    return _final(x1, p, gate2, final_norm_g.reshape(1, d)).reshape(bsz, s, d)
```

```python
import functools

import numpy as np
import jax
import jax.numpy as jnp
from jax import lax
from jax.experimental import pallas as pl
from jax.experimental.pallas import tpu as pltpu

N_ATTN_HEADS = 16
HEAD_DIM = 128
N_IDX_HEADS = 32
IDX_HEAD_DIM = 128
IDX_ROPE_DIM = 64
TOPK_MAX = 256
ROPE_THETA = 10000.0
PEER_TOPK = 16
EPS = 1e-6

LANE = 128
SUBLANE = 8
VMEM_LIMIT = 56 * 1024 * 1024
NEG_BIG = -1e30
LOG2E = 1.4426950408889634
INT_MIN = -2 ** 31
INT_MAX = 2 ** 31 - 1

TM_NORM = 512
TM_MM = 1024
TN_MM = 512
TM_CONV = 256
TQ_DSA = 256
TK_DSA = 1024
TM_MERGE = 512
TN_MERGE = 256
TT_ROUTE = 512
TT_PEER = 512
EC_PEER = 512
SUB_PEER = 256
TM_FINAL = 256

f32 = jnp.float32
bf16 = jnp.bfloat16
i32 = jnp.int32

_NT = (((1,), (1,)), ((), ()))
_TN = (((0,), (0,)), ((), ()))


def _cp(sem, vmem=VMEM_LIMIT, flags=None):
    return pltpu.CompilerParams(dimension_semantics=sem, vmem_limit_bytes=vmem, flags=flags)


_INTERLEAVE = None


def _tile(n, t):
    t = min(n, t)
    assert n % t == 0, (n, t)
    return t


def _ada_kernel(c_ref, w_ref, b_ref, o_ref):
    c = c_ref[...]
    ca = (c * jax.nn.sigmoid(c)).astype(bf16)
    o_ref[...] = jnp.dot(ca, w_ref[...].astype(bf16), preferred_element_type=f32) + b_ref[...]


def _ada(c8, w, b):
    d, n = w.shape
    tn = _tile(n, 512)
    return pl.pallas_call(
        _ada_kernel,
        grid=(n // tn,),
        in_specs=[pl.BlockSpec((SUBLANE, d), lambda j: (0, 0)),
                  pl.BlockSpec((d, tn), lambda j: (0, j)),
                  pl.BlockSpec((1, tn), lambda j: (0, j))],
        out_specs=pl.BlockSpec((SUBLANE, tn), lambda j: (0, j)),
        out_shape=jax.ShapeDtypeStruct((SUBLANE, n), f32),
        compiler_params=_cp(("arbitrary",)),
        name="ada",
    )(c8, w, b)


def _norm_mod_kernel(x_ref, g_ref, sc_ref, sh_ref, o_ref):
    x = x_ref[...]
    y = x * lax.rsqrt(jnp.mean(x * x, axis=-1, keepdims=True) + EPS) * g_ref[...]
    o_ref[...] = (y * (1.0 + sc_ref[...]) + sh_ref[...]).astype(o_ref.dtype)


def _norm_mod_t_kernel(x_ref, g_ref, sc_ref, sh_ref, o_ref):
    x = x_ref[...]
    y = x * lax.rsqrt(jnp.mean(x * x, axis=-1, keepdims=True) + EPS) * g_ref[...]
    o_ref[...] = (y * (1.0 + sc_ref[...]) + sh_ref[...]).T.astype(o_ref.dtype)


def _norm_mod(x, g, scale, shift, transposed=False):
    s, d = x.shape
    tm = _tile(s, TM_NORM)
    row = pl.BlockSpec((1, d), lambda i: (0, 0))
    return pl.pallas_call(
        _norm_mod_t_kernel if transposed else _norm_mod_kernel,
        grid=(s // tm,),
        in_specs=[pl.BlockSpec((tm, d), lambda i: (i, 0)), row, row, row],
        out_specs=pl.BlockSpec((d, tm), lambda i: (0, i)) if transposed else pl.BlockSpec((tm, d), lambda i: (i, 0)),
        out_shape=jax.ShapeDtypeStruct((d, s) if transposed else (s, d), bf16),
        compiler_params=_cp(("arbitrary",)),
        name="norm_mod_t" if transposed else "norm_mod",
    )(x, g, scale, shift)


def _glu_kernel(h_ref, wa_ref, wg_ref, o_ref):
    h = h_ref[...]
    a = jnp.dot(h, wa_ref[...], preferred_element_type=f32)
    g = jnp.dot(h, wg_ref[...], preferred_element_type=f32)
    o_ref[...] = a * jax.nn.sigmoid(g)


def _glu(h, w_glu, cw):
    s, d = h.shape
    tm, tn = _tile(s, TM_MM), _tile(cw, TN_MM)
    nj = cw // tn
    return pl.pallas_call(
        _glu_kernel,
        grid=(s // tm, nj),
        in_specs=[pl.BlockSpec((tm, d), lambda i, j: (i, 0)),
                  pl.BlockSpec((d, tn), lambda i, j: (0, j)),
                  pl.BlockSpec((d, tn), lambda i, j: (0, j + nj))],
        out_specs=pl.BlockSpec((tm, tn), lambda i, j: (i, j)),
        out_shape=jax.ShapeDtypeStruct((s, cw), f32),
        compiler_params=_cp(("arbitrary", "arbitrary")),
        name="glu",
    )(h, w_glu, w_glu)


def _conv_kernel(prev_ref, cur_ref, w_ref, cb_ref, g_ref, b_ref, o_ref, ubuf, ybuf, *, tm, kw, halo, rs):
    i = pl.program_id(0)
    cw = cur_ref.shape[1]
    ubuf[0:halo, :] = jnp.where(i > 0, prev_ref[...], 0.0)
    ubuf[halo:, :] = cur_ref[...]
    off = halo - (kw - 1)

    def row_body(r, carry):
        r0 = pl.multiple_of(r * rs, rs)
        for c in range(cw // LANE):
            cs = slice(c * LANE, (c + 1) * LANE)
            acc = jnp.zeros((rs, LANE), f32) + cb_ref[:, cs]
            win = ubuf[pl.ds(r0, rs + halo), cs]
            for r in range(SUBLANE):
                taps = [k for k in range(kw) if (off + k) % SUBLANE == r]
                if taps:
                    sh = pltpu.roll(win, rs + halo - r, 0) if r else win
                    for k in taps:
                        a = off + k - r
                        acc = acc + w_ref[k:k + 1, cs] * sh[a:a + rs]
            ybuf[pl.ds(r0, rs), cs] = acc
        return carry

    lax.fori_loop(0, tm // rs, row_body, 0)
    y = ybuf[...]
    mu = jnp.mean(y, axis=-1, keepdims=True)
    yc = y - mu
    var = jnp.mean(yc * yc, axis=-1, keepdims=True)
    z = yc * lax.rsqrt(var + EPS) * g_ref[...] + b_ref[...]
    o_ref[...] = (z * jax.nn.sigmoid(z)).astype(o_ref.dtype)


def _conv(u, conv_w, conv_b, ln_g, ln_b):
    s, cw = u.shape
    kw = conv_w.shape[0]
    tm = _tile(s, TM_CONV)
    halo = -(-(kw - 1) // SUBLANE) * SUBLANE
    rs = 64
    assert tm % halo == 0 and tm % rs == 0
    kwp = -(-kw // SUBLANE) * SUBLANE
    w_pad = jnp.zeros((kwp, cw), f32).at[:kw].set(conv_w)
    row = pl.BlockSpec((1, cw), lambda i: (0, 0))
    return pl.pallas_call(
        functools.partial(_conv_kernel, tm=tm, kw=kw, halo=halo, rs=rs),
        grid=(s // tm,),
        in_specs=[pl.BlockSpec((halo, cw), lambda i: (jnp.maximum(i * (tm // halo) - 1, 0), 0)),
                  pl.BlockSpec((tm, cw), lambda i: (i, 0)),
                  pl.BlockSpec((kwp, cw), lambda i: (0, 0)), row, row, row],
        out_specs=pl.BlockSpec((tm, cw), lambda i: (i, 0)),
        out_shape=jax.ShapeDtypeStruct((s, cw), bf16),
        scratch_shapes=[pltpu.VMEM((halo + tm, cw), f32), pltpu.VMEM((tm, cw), f32)],
        compiler_params=_cp(("arbitrary",)),
        name="conv",
    )(u, u, w_pad, conv_b, ln_g, ln_b)


def _proj_t_kernel(w_ref, h_ref, cf_ref, sf_ref, cp_ref, sp_ref, o_ref, *, n_idx_tiles, n_rope_tiles, tn, q_mult):
    j = pl.program_id(1)
    acc = lax.dot_general(w_ref[...], h_ref[...], _NT, preferred_element_type=f32)

    def rope_rows(cos_ref, sin_ref, rot, mult=None):
        half = rot // 2
        cos, sin = cos_ref[...], sin_ref[...]
        if mult is not None:
            cos, sin = cos * mult, sin * mult
        for hd in range(tn // LANE):
            r0 = hd * LANE
            x1 = acc[r0:r0 + half]
            x2 = acc[r0 + half:r0 + rot]
            o_ref[r0:r0 + half, :] = (x1 * cos - x2 * sin).astype(o_ref.dtype)
            o_ref[r0 + half:r0 + rot, :] = (x2 * cos + x1 * sin).astype(o_ref.dtype)
            if rot < LANE:
                o_ref[r0 + rot:r0 + LANE, :] = acc[r0 + rot:r0 + LANE].astype(o_ref.dtype)

    @pl.when(j < n_idx_tiles)
    def _():
        rope_rows(cp_ref, sp_ref, IDX_ROPE_DIM)

    @pl.when((j >= n_idx_tiles) & (j < n_idx_tiles + n_rope_tiles))
    def _():
        rope_rows(cf_ref, sf_ref, HEAD_DIM, q_mult)

    @pl.when(j >= n_idx_tiles + n_rope_tiles)
    def _():
        o_ref[...] = acc.astype(o_ref.dtype)


def _proj_t(wt, h, cos_f, sin_f, cos_p, sin_p, n_idx_rows, n_rope_rows, q_mult):
    n, d = wt.shape
    s = h.shape[0]
    tm, tn = _tile(s, TM_MM), _tile(n_rope_rows, TN_MM)
    assert n % tn == 0 and n_idx_rows % tn == 0 and n_rope_rows % tn == 0
    kern = functools.partial(_proj_t_kernel, n_idx_tiles=n_idx_rows // tn, n_rope_tiles=n_rope_rows // tn, tn=tn,
                             q_mult=q_mult)
    tab = lambda t: pl.BlockSpec((t.shape[0], tm), lambda i, j: (0, i))
    return pl.pallas_call(
        kern,
        grid=(s // tm, n // tn),
        in_specs=[pl.BlockSpec((tn, d), lambda i, j: (j, 0)),
                  pl.BlockSpec((tm, d), lambda i, j: (i, 0)),
                  tab(cos_f), tab(sin_f), tab(cos_p), tab(sin_p)],
        out_specs=pl.BlockSpec((tn, tm), lambda i, j: (j, i)),
        out_shape=jax.ShapeDtypeStruct((n, s), bf16),
        compiler_params=_cp(("arbitrary", "arbitrary")),
        name="proj_t",
    )(wt, h, cos_f, sin_f, cos_p, sin_p)


def _proj_k_kernel(h_ref, w_ref, c_ref, s_ref, o_ref, *, tn):
    acc = jnp.dot(h_ref[...], w_ref[...], preferred_element_type=f32)
    cos, sin = c_ref[...], s_ref[...]
    for hd in range(tn // LANE):
        x = acc[:, hd * LANE:(hd + 1) * LANE]
        y = x * cos + pltpu.roll(x, HEAD_DIM // 2, 1) * sin
        o_ref[:, hd * LANE:(hd + 1) * LANE] = y.astype(o_ref.dtype)


def _proj_k(h, w, cos_tok, sin_tok):
    s, d = h.shape
    n = w.shape[1]
    tm, tn = _tile(s, TM_MM), _tile(n, TN_MM)
    tab = pl.BlockSpec((tm, LANE), lambda i, j: (i, 0))
    return pl.pallas_call(
        functools.partial(_proj_k_kernel, tn=tn),
        grid=(s // tm, n // tn),
        in_specs=[pl.BlockSpec((tm, d), lambda i, j: (i, 0)),
                  pl.BlockSpec((d, tn), lambda i, j: (0, j)), tab, tab],
        out_specs=pl.BlockSpec((tm, tn), lambda i, j: (i, j)),
        out_shape=jax.ShapeDtypeStruct((s, n), bf16),
        compiler_params=_cp(("arbitrary", "arbitrary")),
        name="proj_k",
    )(h, w, cos_tok, sin_tok)


def _kw_kernel(h_ref, w_ref, g_ref, b_ref, c_ref, s1_ref, s2_ref, kidx_ref, widx_ref, *, w_scale):
    acc = jnp.dot(h_ref[...], w_ref[...], preferred_element_type=f32)
    k = acc[:, :LANE]
    mu = jnp.mean(k, axis=-1, keepdims=True)
    kc = k - mu
    var = jnp.mean(kc * kc, axis=-1, keepdims=True)
    y = kc * lax.rsqrt(var + EPS) * g_ref[...] + b_ref[...]
    half = IDX_ROPE_DIM // 2
    y = y * c_ref[...] + pltpu.roll(y, half, 1) * s1_ref[...] + pltpu.roll(y, LANE - half, 1) * s2_ref[...]
    kidx_ref[...] = y.astype(kidx_ref.dtype)
    widx_ref[...] = (acc[:, LANE:] * w_scale).T


def _kw(h, w_kw, ln_g, ln_b, c_tok, s1_tok, s2_tok, w_scale):
    s, d = h.shape
    tm = _tile(s, TM_NORM)
    row = pl.BlockSpec((1, LANE), lambda i: (0, 0))
    tab = pl.BlockSpec((tm, LANE), lambda i: (i, 0))
    return pl.pallas_call(
        functools.partial(_kw_kernel, w_scale=w_scale),
        grid=(s // tm,),
        in_specs=[pl.BlockSpec((tm, d), lambda i: (i, 0)),
                  pl.BlockSpec((d, 2 * LANE), lambda i: (0, 0)), row, row, tab, tab, tab],
        out_specs=[pl.BlockSpec((tm, LANE), lambda i: (i, 0)),
                   pl.BlockSpec((LANE, tm), lambda i: (0, i))],
        out_shape=[jax.ShapeDtypeStruct((s, LANE), bf16), jax.ShapeDtypeStruct((LANE, s), f32)],
        compiler_params=_cp(("arbitrary",)),
        name="kw",
    )(h, w_kw, ln_g, ln_b, c_tok, s1_tok, s2_tok)


def _int_key(x):
    bits = pltpu.bitcast(x, i32)
    return bits ^ ((bits >> 31) & 0x7FFFFFFF)


def _dsa_kernel(qidx_ref, q_ref, k_ref, v_ref, kidx_ref, w_ref, o_ref,
                key_ref, thr_ref, m_ref, l_ref, acc_ref, gm_ref, *, tq, tk, n_heads, n_idx_heads, k_sel, n_split,
                n_grp):
    i = pl.program_id(0)
    j = pl.program_id(1)
    last = ((i + 1) * tq - 1) // tk
    nkb = last + 1
    neg_key = int(np.array(-np.inf, np.float32).view(np.int32)) ^ 0x7FFFFFFF

    @pl.when(j == 0)
    def _score_and_select():
        m_ref[...] = jnp.full(m_ref.shape, NEG_BIG, f32)
        l_ref[...] = jnp.zeros(l_ref.shape, f32)
        acc_ref[...] = jnp.zeros(acc_ref.shape, f32)
        qpos = i * tq + lax.broadcasted_iota(i32, (tk, tq), 1)

        def score_body(c, carry):
            kc = kidx_ref[pl.ds(pl.multiple_of(c * tk, tk), tk), :]
            acc = jnp.zeros((tk, tq), f32)
            for h in range(n_idx_heads):
                lg = jnp.dot(kc, qidx_ref[h * LANE:(h + 1) * LANE, :], preferred_element_type=f32)
                acc = acc + w_ref[h:h + 1, :] * jnp.maximum(lg, 0.0)
            kpos = c * tk + lax.broadcasted_iota(i32, (tk, tq), 0)
            acc = jnp.where(kpos <= qpos, acc, -jnp.inf)
            key_ref[c] = _int_key(acc)
            gm = gm_ref[...]
            for r in range(tk // n_grp):
                gm = jnp.maximum(gm, acc[r * n_grp:(r + 1) * n_grp])
            gm_ref[...] = gm
            return carry

        gm_ref[...] = jnp.full(gm_ref.shape, -jnp.inf, f32)
        lax.fori_loop(0, nkb, score_body, 0)

        def bis_body(it, carry):
            lo, hi = carry
            mid = (lo >> 1) + (hi >> 1) + (lo & hi & 1)

            def cnt_body(c, cnt):
                parts = [cnt, jnp.zeros_like(cnt), jnp.zeros_like(cnt), jnp.zeros_like(cnt)]
                for g in range(tk // SUBLANE):
                    blk = key_ref[c, g * SUBLANE:(g + 1) * SUBLANE, :]
                    parts[g % 4] = parts[g % 4] + jnp.where(blk >= mid, 1.0, 0.0)
                return (parts[0] + parts[1]) + (parts[2] + parts[3])

            cnt = lax.fori_loop(0, nkb, cnt_body, jnp.zeros((SUBLANE, tq), f32))
            tot = jnp.broadcast_to(jnp.sum(cnt, axis=0, keepdims=True), (SUBLANE, tq))
            ge = tot >= float(k_sel)
            return jnp.where(ge, mid, lo), jnp.where(ge, hi, mid)

        gm = gm_ref[...]
        lo0 = jnp.broadcast_to(_int_key(jnp.min(gm, axis=0, keepdims=True)), (SUBLANE, tq)) - 1
        hi0 = jnp.broadcast_to(_int_key(jnp.max(gm, axis=0, keepdims=True)), (SUBLANE, tq)) + 2
        span = hi0.astype(f32) - lo0.astype(f32)
        n_bits = ((pltpu.bitcast(span, i32) >> 23) - 125).astype(f32)
        n_iter = jnp.clip(jnp.max(n_bits).astype(i32), 1, 32)
        lo, _ = lax.fori_loop(0, n_iter, bis_body, (lo0, hi0))
        thr_ref[...] = jnp.maximum(lo, neg_key + 1)

    @pl.when(j <= last)
    def _attend():
        parts = [slice(r * (tk // n_split), (r + 1) * (tk // n_split)) for r in range(n_split)]
        bias = [jnp.where(key_ref[j, ks, :] >= thr_ref[0:1, :], 0.0, NEG_BIG) for ks in parts]
        for h in range(n_heads):
            hs = slice(h * HEAD_DIM, (h + 1) * HEAD_DIM)
            q = q_ref[hs, :]
            sts = [jnp.dot(k_ref[ks, hs], q, preferred_element_type=f32) + bs for ks, bs in zip(parts, bias)]
            m_blk = functools.reduce(jnp.maximum, [jnp.max(st, axis=0, keepdims=True) for st in sts])
            m_old = m_ref[h]
            m_new = jnp.maximum(m_old, m_blk)
            alpha = jnp.exp2(m_old - m_new)
            ps = [jnp.exp2(st - m_new[0:1, :]) for st in sts]
            l_ref[h] = alpha * l_ref[h] + sum(jnp.sum(p, axis=0, keepdims=True) for p in ps)
            pv = sum(jnp.dot(v_ref[hs, ks], p.astype(bf16), preferred_element_type=f32)
                     for ks, p in zip(parts, ps))
            acc_ref[hs, :] = alpha[0:1, :] * acc_ref[hs, :] + pv
            m_ref[h] = m_new

    @pl.when(j == last)
    def _finish():
        for h in range(n_heads):
            hs = slice(h * HEAD_DIM, (h + 1) * HEAD_DIM)
            out_t = acc_ref[hs, :] / l_ref[h][0:1, :]
            o_ref[:, hs] = out_t.T.astype(o_ref.dtype)


def _dsa(feat_t, k_tok, kidx, widx_t, n_idx_rows, attn_w, k_sel):
    s = k_tok.shape[0]
    tq, tk = _tile(s, TQ_DSA), _tile(s, TK_DSA)
    n_grp = -(-k_sel // SUBLANE) * SUBLANE
    assert tk % n_grp == 0 and n_idx_rows % attn_w == 0 and tk % tq == 0
    nq, nk = s // tq, s // tk
    qb = n_idx_rows // attn_w
    last = lambda i: ((i + 1) * tq - 1) // tk
    kern = functools.partial(_dsa_kernel, tq=tq, tk=tk, n_heads=attn_w // HEAD_DIM,
                             n_idx_heads=n_idx_rows // IDX_HEAD_DIM, k_sel=k_sel, n_split=8, n_grp=n_grp)
    return pl.pallas_call(
        kern,
        grid=(nq, nk),
        in_specs=[pl.BlockSpec((n_idx_rows, tq), lambda i, j: (0, i)),
                  pl.BlockSpec((attn_w, tq), lambda i, j: (qb, i)),
                  pl.BlockSpec((tk, attn_w), lambda i, j: (jnp.minimum(j, last(i)), 0)),
                  pl.BlockSpec((attn_w, tk), lambda i, j: (qb + 1, jnp.minimum(j, last(i)))),
                  pl.BlockSpec((s, LANE), lambda i, j: (0, 0)),
                  pl.BlockSpec((LANE, tq), lambda i, j: (0, i))],
        out_specs=pl.BlockSpec((tq, attn_w), lambda i, j: (i, 0)),
        out_shape=jax.ShapeDtypeStruct((s, attn_w), bf16),
        scratch_shapes=[pltpu.VMEM((nk, tk, tq), i32),
                        pltpu.VMEM((SUBLANE, tq), i32),
                        pltpu.VMEM((attn_w // HEAD_DIM, SUBLANE, tq), f32),
                        pltpu.VMEM((attn_w // HEAD_DIM, SUBLANE, tq), f32),
                        pltpu.VMEM((attn_w, tq), f32),
                        pltpu.VMEM((n_grp, tq), f32)],
        compiler_params=_cp(("arbitrary", "arbitrary")),
        name="dsa",
    )(feat_t, feat_t, k_tok, feat_t, kidx, widx_t)


def _merge_kernel(h_ref, a_ref, b_ref, wg0_ref, wg1_ref, wo0_ref, wo1_ref, x_ref, g_ref, o_ref):
    h = h_ref[...]
    g0 = jax.nn.sigmoid(jnp.dot(h, wg0_ref[...], preferred_element_type=f32))
    g1 = jax.nn.sigmoid(jnp.dot(h, wg1_ref[...], preferred_element_type=f32))
    ya = jnp.dot(a_ref[...], wo0_ref[...], preferred_element_type=f32)
    yb = jnp.dot(b_ref[...], wo1_ref[...], preferred_element_type=f32)
    o_ref[...] = x_ref[...] + g_ref[...] * (g0 * ya + g1 * yb)


def _merge(h, a_out, b_out, w_gate, w_out, x, gate1):
    s, d = x.shape
    cw = a_out.shape[1]
    assert b_out.shape[1] == cw
    tm, tn = _tile(s, TM_MERGE), _tile(d, TN_MERGE)
    nj = d // tn
    return pl.pallas_call(
        _merge_kernel,
        grid=(s // tm, nj),
        in_specs=[pl.BlockSpec((tm, d), lambda i, j: (i, 0)),
                  pl.BlockSpec((tm, cw), lambda i, j: (i, 0)),
                  pl.BlockSpec((tm, cw), lambda i, j: (i, 0)),
                  pl.BlockSpec((d, tn), lambda i, j: (0, j)),
                  pl.BlockSpec((d, tn), lambda i, j: (0, j + nj)),
                  pl.BlockSpec((cw, tn), lambda i, j: (0, j)),
                  pl.BlockSpec((cw, tn), lambda i, j: (1, j)),
                  pl.BlockSpec((tm, tn), lambda i, j: (i, j)),
                  pl.BlockSpec((1, tn), lambda i, j: (0, j))],
        out_specs=pl.BlockSpec((tm, tn), lambda i, j: (i, j)),
        out_shape=jax.ShapeDtypeStruct((s, d), f32),
        compiler_params=_cp(("arbitrary", "arbitrary")),
        name="merge",
    )(h, a_out, b_out, w_gate, w_gate, w_out, w_out, x, gate1)


def _top_sorted(x, k):
    n = x.shape[0]
    rows = lax.broadcasted_iota(i32, x.shape, 0).astype(f32)
    cur, tops = x, []
    for _ in range(k):
        m = jnp.max(cur, axis=0, keepdims=True)
        first = jnp.min(jnp.where(cur == m, rows, float(n)), axis=0, keepdims=True)
        cur = jnp.where(rows == first, -jnp.inf, cur)
        tops.append(m)
    return tops


def _route_kernel(h_ref, wq_ref, sk_ref, a_ref, b_ref, ea_ref, eb_ref, thr_ref, *, half, topk):
    qt = jnp.dot(wq_ref[...], h_ref[...], preferred_element_type=f32).astype(bf16)
    a = jnp.dot(sk_ref[0, 0], qt[:half], preferred_element_type=f32)
    b = jnp.dot(sk_ref[0, 1], qt[half:], preferred_element_type=f32)
    ta = _top_sorted(a, topk)
    tb = jnp.concatenate(_top_sorted(b, topk), axis=0)
    n_q = lambda p: -(-(topk // (p + 1)) // SUBLANE) * SUBLANE
    cand = jnp.concatenate([t + tb[:n_q(p)] for p, t in enumerate(ta)], axis=0)
    best = _top_sorted(cand, topk)
    thr = best[-1]
    z = jnp.ones_like(thr)
    for t in best[1:]:
        z = z + jnp.exp(t - best[0])
    a_ref[0] = a
    b_ref[0] = b
    ea_ref[0] = jnp.exp(a - ta[0])
    eb_ref[0] = jnp.exp(b - tb[0:1]) / z
    thr_ref[0] = jnp.broadcast_to(thr, thr_ref.shape[1:])


def _route(h2t, wq_t, sub_keys):
    d, s = h2t.shape
    heads, _, n_keys, half = sub_keys.shape
    tt = _tile(s, TT_ROUTE)
    big = pl.BlockSpec((1, n_keys, tt), lambda t, h: (h, 0, t))
    shp = jax.ShapeDtypeStruct((heads, n_keys, s), f32)
    return pl.pallas_call(
        functools.partial(_route_kernel, half=half, topk=PEER_TOPK),
        grid=(s // tt, heads),
        in_specs=[pl.BlockSpec((d, tt), lambda t, h: (0, t)),
                  pl.BlockSpec((2 * half, d), lambda t, h: (h, 0)),
                  pl.BlockSpec((1, 2, n_keys, half), lambda t, h: (h, 0, 0, 0))],
        out_specs=[big, big, big, big, pl.BlockSpec((1, SUBLANE, tt), lambda t, h: (h, 0, t))],
        out_shape=[shp, shp, shp, shp, jax.ShapeDtypeStruct((heads, SUBLANE, s), f32)],
        compiler_params=_cp(("arbitrary", "arbitrary")),
        name="route",
    )(h2t, wq_t, sub_keys)


def _peer_kernel(h_ref, u_ref, v_ref, a_ref, b_ref, ea_ref, eb_ref, thr_ref, o_ref, at_ref, g_ref, wt_ref,
                 *, ec, sub, heads, n_keys, lw):
    e = pl.program_id(1)
    d, tt = h_ref.shape
    nsub = ec // sub
    pieces = [(ii, lh) for ii in range(sub // n_keys) for lh in range(tt // lw)]
    dn = d // len(pieces)

    @pl.when(e == 0)
    def _():
        o_ref[...] = jnp.zeros(o_ref.shape, f32)

    def gate_piece(c, ii, lh):
        row = e * (ec // n_keys) + c * (sub // n_keys) + ii
        rs = slice(c * sub + ii * n_keys, c * sub + (ii + 1) * n_keys)
        ls = slice(lh * lw, (lh + 1) * lw)
        g = jnp.zeros((n_keys, lw), f32)
        for hd in range(heads):
            val = a_ref[hd, pl.ds(row, 1), :][:, ls] + b_ref[hd, :, ls]
            g = g + jnp.where(val >= thr_ref[hd, 0:1, ls],
                              ea_ref[hd, pl.ds(row, 1), :][:, ls] * eb_ref[hd, :, ls], 0.0)
        g_ref[rs, ls] = g

    def combine(c):
        for ii, lh in pieces:
            rs = slice(c * sub + ii * n_keys, c * sub + (ii + 1) * n_keys)
            ls = slice(lh * lw, (lh + 1) * lw)
            at = at_ref[rs, ls]
            act = 0.5 * at * (1.0 + lax.erf(at * np.float32(np.sqrt(0.5))))
            wt_ref[ls, rs] = (g_ref[rs, ls] * act).T.astype(bf16)

    def out_chunk(c, k):
        cs = slice(c * sub, (c + 1) * sub)
        ns = slice(k * dn, (k + 1) * dn)
        o_ref[:, ns] += jnp.dot(wt_ref[:, cs], v_ref[cs, ns], preferred_element_type=f32)

    h = h_ref[...]
    for p in pieces:
        gate_piece(0, *p)
    for c in range(nsub):
        cs = slice(c * sub, (c + 1) * sub)
        at_ref[cs, :] = jnp.dot(u_ref[cs, :], h, preferred_element_type=f32)
    for c in range(nsub):
        combine(c)
        for k, p in enumerate(pieces):
            out_chunk(c, k)
            if c + 1 < nsub:
                gate_piece(c + 1, *p)


def _peer(h2t, u, v, a, b, ea, eb, thr):
    d, s = h2t.shape
    n_exp = u.shape[0]
    heads, n_keys, _ = a.shape
    tt, ec = _tile(s, TT_PEER), _tile(n_exp, EC_PEER)
    sub = min(ec, SUB_PEER)
    assert ec % sub == 0 and sub % n_keys == 0
    once = pl.Buffered(1)
    tab = pl.BlockSpec((heads, n_keys, tt), lambda t, e: (0, 0, t), pipeline_mode=once)
    return pl.pallas_call(
        functools.partial(_peer_kernel, ec=ec, sub=sub, heads=heads, n_keys=n_keys, lw=LANE),
        grid=(s // tt, n_exp // ec),
        in_specs=[pl.BlockSpec((d, tt), lambda t, e: (0, t), pipeline_mode=once),
                  pl.BlockSpec((ec, d), lambda t, e: (e, 0)),
                  pl.BlockSpec((ec, d), lambda t, e: (e, 0)),
                  tab, tab, tab, tab,
                  pl.BlockSpec((heads, SUBLANE, tt), lambda t, e: (0, 0, t), pipeline_mode=once)],
        out_specs=pl.BlockSpec((tt, d), lambda t, e: (t, 0)),
        out_shape=jax.ShapeDtypeStruct((s, d), f32),
        scratch_shapes=[pltpu.VMEM((ec, tt), f32), pltpu.VMEM((ec, tt), f32), pltpu.VMEM((tt, ec), bf16)],
        compiler_params=_cp(("arbitrary", "arbitrary"), flags=_INTERLEAVE),
        name="peer",
    )(h2t, u, v, a, b, ea, eb, thr)


def _final_kernel(x_ref, p_ref, g2_ref, fg_ref, o_ref):
    x = x_ref[...] + g2_ref[...] * p_ref[...]
    o_ref[...] = x * lax.rsqrt(jnp.mean(x * x, axis=-1, keepdims=True) + EPS) * fg_ref[...]


def _final(x1, p, gate2, fg):
    s, d = x1.shape
    tm = _tile(s, TM_FINAL)
    row = pl.BlockSpec((1, d), lambda i: (0, 0))
    blk = pl.BlockSpec((tm, d), lambda i: (i, 0))
    return pl.pallas_call(
        _final_kernel,
        grid=(s // tm,),
        in_specs=[blk, blk, row, row],
        out_specs=blk,
        out_shape=jax.ShapeDtypeStruct((s, d), f32),
        compiler_params=_cp(("arbitrary",)),
        name="final",
    )(x1, p, gate2, fg)


def _rope_tables(positions, dim):
    inv_freq = ROPE_THETA ** (-jnp.arange(0, dim, 2, dtype=f32) / dim)
    ang = positions.astype(f32)[:, None] * inv_freq
    return jnp.cos(ang), jnp.sin(ang)


def kernel(x, c, positions, w_ada, b_ada, norm_mix_g, norm_ffn_g, w_in, conv_w, conv_b, conv_ln_g, conv_ln_b,
           k_idx_ln_g, k_idx_ln_b, w_out, peer_w_q, peer_sub_keys, peer_u, peer_v, final_norm_g):
    bsz, s, d = x.shape
    depth = w_ada.shape[0]
    assert bsz == 1 and depth == 1, "one sequence, one layer"
    cw = conv_w.shape[-1]
    aw = N_ATTN_HEADS * HEAD_DIM
    iw = N_IDX_HEADS * IDX_HEAD_DIM
    assert cw == aw
    x2d = x.reshape(s, d)
    pos = positions.reshape(s)

    w_in2 = w_in.reshape(d, -1)
    o_q, o_k, o_v, o_qi = 2 * cw, 2 * cw + aw, 2 * cw + 2 * aw, 2 * cw + 3 * aw
    o_ki = o_qi + iw
    o_wi = o_ki + IDX_HEAD_DIM
    o_g = o_wi + N_IDX_HEADS
    w_glu = w_in2[:, :o_q].astype(bf16)
    w_feat_t = jnp.concatenate([w_in2[:, o_qi:o_ki].astype(bf16), w_in2[:, o_q:o_k].astype(bf16),
                                w_in2[:, o_v:o_qi].astype(bf16)], axis=1).T
    w_k = w_in2[:, o_k:o_v].astype(bf16)
    w_kw = w_in2[:, o_ki:o_ki + 2 * LANE].astype(bf16)
    w_gate = w_in2[:, o_g:].astype(bf16)
    w_out2 = w_out.reshape(cw + aw, d).astype(bf16)
    wq_t = peer_w_q.reshape(d, -1).astype(bf16).T
    sub_keys = peer_sub_keys.reshape(peer_sub_keys.shape[1:]).astype(bf16)
    u_exp = peer_u.reshape(-1, d).astype(bf16)
    v_exp = peer_v.reshape(-1, d).astype(bf16)

    cos_f, sin_f = _rope_tables(pos, HEAD_DIM)
    cos_p, sin_p = _rope_tables(pos, IDX_ROPE_DIM)
    cos_tok = jnp.concatenate([cos_f, cos_f], axis=1)
    sin_tok = jnp.concatenate([-sin_f, sin_f], axis=1)
    zeros_p = jnp.zeros_like(sin_p)
    pad = jnp.zeros((s, LANE - IDX_ROPE_DIM), f32)
    c_tok = jnp.concatenate([cos_p, cos_p, pad + 1.0], axis=1)
    s1_tok = jnp.concatenate([zeros_p, sin_p, pad], axis=1)
    s2_tok = jnp.concatenate([-sin_p, zeros_p, pad], axis=1)

    ada = _ada(jnp.broadcast_to(c.reshape(1, d), (SUBLANE, d)), w_ada.reshape(d, -1), b_ada.reshape(1, -1))[0:1]
    shift1, scale1, gate1, shift2, scale2, gate2 = jnp.split(ada, 6, axis=-1)

    h = _norm_mod(x2d, norm_mix_g.reshape(1, d), scale1, shift1)
    u = _glu(h, w_glu, cw)
    a_out = _conv(u, conv_w.reshape(-1, cw), conv_b.reshape(1, cw), conv_ln_g.reshape(1, cw), conv_ln_b.reshape(1, cw))
    feat_t = _proj_t(w_feat_t, h, cos_f.T, sin_f.T, cos_p.T, sin_p.T, iw, aw, HEAD_DIM ** -0.5 * LOG2E)
    k_tok = _proj_k(h, w_k, cos_tok, sin_tok)
    w_scale = float(N_IDX_HEADS ** -0.5 * IDX_HEAD_DIM ** -0.5)
    kidx, widx_t = _kw(h, w_kw, k_idx_ln_g.reshape(1, -1), k_idx_ln_b.reshape(1, -1), c_tok, s1_tok, s2_tok, w_scale)
    b_out = _dsa(feat_t, k_tok, kidx, widx_t, iw, aw, min(TOPK_MAX, s // 4))
    x1 = _merge(h, a_out, b_out, w_gate, w_out2, x2d, gate1)

    h2t = _norm_mod(x1, norm_ffn_g.reshape(1, d), scale2, shift2, transposed=True)
    a, b, ea, eb, thr = _route(h2t, wq_t, sub_keys)
    p = _peer(h2t, u_exp, v_exp, a, b, ea, eb, thr)
    return _final(x1, p, gate2, final_norm_g.reshape(1, d)).reshape(bsz, s, d)
```

```python
import functools

import numpy as np
import jax
import jax.numpy as jnp
from jax import lax
from jax.experimental import pallas as pl
from jax.experimental.pallas import tpu as pltpu

N_ATTN_HEADS = 16
HEAD_DIM = 128
N_IDX_HEADS = 32
IDX_HEAD_DIM = 128
IDX_ROPE_DIM = 64
TOPK_MAX = 256
ROPE_THETA = 10000.0
PEER_TOPK = 16
EPS = 1e-6

LANE = 128
SUBLANE = 8
VMEM_LIMIT = 56 * 1024 * 1024
NEG_BIG = -1e30
LOG2E = 1.4426950408889634
INT_MIN = -2 ** 31
INT_MAX = 2 ** 31 - 1
N_VALUE_PASSES = 12
MAX_BISECT = N_VALUE_PASSES + 33

TM_NORM = 512
TM_MM = 1024
TN_MM = 512
TM_CONV = 256
TQ_DSA = 256
TK_DSA = 1024
TM_MERGE = 512
TN_MERGE = 256
TT_ROUTE = 512
TT_PEER = 512
EC_PEER = 512
SUB_PEER = 256
TM_FINAL = 256

f32 = jnp.float32
bf16 = jnp.bfloat16
i32 = jnp.int32

_NT = (((1,), (1,)), ((), ()))
_TN = (((0,), (0,)), ((), ()))


def _cp(sem, vmem=VMEM_LIMIT):
    return pltpu.CompilerParams(dimension_semantics=sem, vmem_limit_bytes=vmem)


def _tile(n, t):
    t = min(n, t)
    assert n % t == 0, (n, t)
    return t


def _ada_kernel(c_ref, w_ref, b_ref, o_ref):
    c = c_ref[...]
    ca = (c * jax.nn.sigmoid(c)).astype(bf16)
    o_ref[...] = jnp.dot(ca, w_ref[...].astype(bf16), preferred_element_type=f32) + b_ref[...]


def _ada(c8, w, b):
    d, n = w.shape
    tn = _tile(n, 512)
    return pl.pallas_call(
        _ada_kernel,
        grid=(n // tn,),
        in_specs=[pl.BlockSpec((SUBLANE, d), lambda j: (0, 0)),
                  pl.BlockSpec((d, tn), lambda j: (0, j)),
                  pl.BlockSpec((1, tn), lambda j: (0, j))],
        out_specs=pl.BlockSpec((SUBLANE, tn), lambda j: (0, j)),
        out_shape=jax.ShapeDtypeStruct((SUBLANE, n), f32),
        compiler_params=_cp(("arbitrary",)),
        name="ada",
    )(c8, w, b)


def _norm_mod_kernel(x_ref, g_ref, sc_ref, sh_ref, o_ref):
    x = x_ref[...]
    y = x * lax.rsqrt(jnp.mean(x * x, axis=-1, keepdims=True) + EPS) * g_ref[...]
    o_ref[...] = (y * (1.0 + sc_ref[...]) + sh_ref[...]).astype(o_ref.dtype)


def _norm_mod_t_kernel(x_ref, g_ref, sc_ref, sh_ref, o_ref):
    x = x_ref[...]
    y = x * lax.rsqrt(jnp.mean(x * x, axis=-1, keepdims=True) + EPS) * g_ref[...]
    o_ref[...] = (y * (1.0 + sc_ref[...]) + sh_ref[...]).T.astype(o_ref.dtype)


def _norm_mod(x, g, scale, shift, transposed=False):
    s, d = x.shape
    tm = _tile(s, TM_NORM)
    row = pl.BlockSpec((1, d), lambda i: (0, 0))
    return pl.pallas_call(
        _norm_mod_t_kernel if transposed else _norm_mod_kernel,
        grid=(s // tm,),
        in_specs=[pl.BlockSpec((tm, d), lambda i: (i, 0)), row, row, row],
        out_specs=pl.BlockSpec((d, tm), lambda i: (0, i)) if transposed else pl.BlockSpec((tm, d), lambda i: (i, 0)),
        out_shape=jax.ShapeDtypeStruct((d, s) if transposed else (s, d), bf16),
        compiler_params=_cp(("arbitrary",)),
        name="norm_mod_t" if transposed else "norm_mod",
    )(x, g, scale, shift)


def _glu_kernel(h_ref, wa_ref, wg_ref, o_ref):
    h = h_ref[...]
    a = jnp.dot(h, wa_ref[...], preferred_element_type=f32)
    g = jnp.dot(h, wg_ref[...], preferred_element_type=f32)
    o_ref[...] = a * jax.nn.sigmoid(g)


def _glu(h, w_glu, cw):
    s, d = h.shape
    tm, tn = _tile(s, TM_MM), _tile(cw, TN_MM)
    nj = cw // tn
    return pl.pallas_call(
        _glu_kernel,
        grid=(s // tm, nj),
        in_specs=[pl.BlockSpec((tm, d), lambda i, j: (i, 0)),
                  pl.BlockSpec((d, tn), lambda i, j: (0, j)),
                  pl.BlockSpec((d, tn), lambda i, j: (0, j + nj))],
        out_specs=pl.BlockSpec((tm, tn), lambda i, j: (i, j)),
        out_shape=jax.ShapeDtypeStruct((s, cw), f32),
        compiler_params=_cp(("arbitrary", "arbitrary")),
        name="glu",
    )(h, w_glu, w_glu)


def _conv_kernel(prev_ref, cur_ref, w_ref, cb_ref, g_ref, b_ref, o_ref, ubuf, ybuf, *, tm, kw, halo, rs):
    i = pl.program_id(0)
    cw = cur_ref.shape[1]
    ubuf[0:halo, :] = jnp.where(i > 0, prev_ref[...], 0.0)
    ubuf[halo:, :] = cur_ref[...]
    off = halo - (kw - 1)

    def row_body(r, carry):
        r0 = pl.multiple_of(r * rs, rs)
        for c in range(cw // LANE):
            cs = slice(c * LANE, (c + 1) * LANE)
            acc = jnp.zeros((rs, LANE), f32) + cb_ref[:, cs]
            win = ubuf[pl.ds(r0, rs + halo), cs]
            for r in range(SUBLANE):
                taps = [k for k in range(kw) if (off + k) % SUBLANE == r]
                if taps:
                    sh = pltpu.roll(win, rs + halo - r, 0) if r else win
                    for k in taps:
                        a = off + k - r
                        acc = acc + w_ref[k:k + 1, cs] * sh[a:a + rs]
            ybuf[pl.ds(r0, rs), cs] = acc
        return carry

    lax.fori_loop(0, tm // rs, row_body, 0)
    y = ybuf[...]
    mu = jnp.mean(y, axis=-1, keepdims=True)
    yc = y - mu
    var = jnp.mean(yc * yc, axis=-1, keepdims=True)
    z = yc * lax.rsqrt(var + EPS) * g_ref[...] + b_ref[...]
    o_ref[...] = (z * jax.nn.sigmoid(z)).astype(o_ref.dtype)


def _conv(u, conv_w, conv_b, ln_g, ln_b):
    s, cw = u.shape
    kw = conv_w.shape[0]
    tm = _tile(s, TM_CONV)
    halo = -(-(kw - 1) // SUBLANE) * SUBLANE
    rs = 64
    assert tm % halo == 0 and tm % rs == 0
    kwp = -(-kw // SUBLANE) * SUBLANE
    w_pad = jnp.zeros((kwp, cw), f32).at[:kw].set(conv_w)
    row = pl.BlockSpec((1, cw), lambda i: (0, 0))
    return pl.pallas_call(
        functools.partial(_conv_kernel, tm=tm, kw=kw, halo=halo, rs=rs),
        grid=(s // tm,),
        in_specs=[pl.BlockSpec((halo, cw), lambda i: (jnp.maximum(i * (tm // halo) - 1, 0), 0)),
                  pl.BlockSpec((tm, cw), lambda i: (i, 0)),
                  pl.BlockSpec((kwp, cw), lambda i: (0, 0)), row, row, row],
        out_specs=pl.BlockSpec((tm, cw), lambda i: (i, 0)),
        out_shape=jax.ShapeDtypeStruct((s, cw), bf16),
        scratch_shapes=[pltpu.VMEM((halo + tm, cw), f32), pltpu.VMEM((tm, cw), f32)],
        compiler_params=_cp(("arbitrary",)),
        name="conv",
    )(u, u, w_pad, conv_b, ln_g, ln_b)


def _proj_t_kernel(w_ref, h_ref, cf_ref, sf_ref, cp_ref, sp_ref, o_ref, *, n_idx_tiles, n_rope_tiles, tn, q_mult):
    j = pl.program_id(1)
    acc = lax.dot_general(w_ref[...], h_ref[...], _NT, preferred_element_type=f32)

    def rope_rows(cos_ref, sin_ref, rot, mult=None):
        half = rot // 2
        cos, sin = cos_ref[...], sin_ref[...]
        if mult is not None:
            cos, sin = cos * mult, sin * mult
        for hd in range(tn // LANE):
            r0 = hd * LANE
            x1 = acc[r0:r0 + half]
            x2 = acc[r0 + half:r0 + rot]
            o_ref[r0:r0 + half, :] = (x1 * cos - x2 * sin).astype(o_ref.dtype)
            o_ref[r0 + half:r0 + rot, :] = (x2 * cos + x1 * sin).astype(o_ref.dtype)
            if rot < LANE:
                o_ref[r0 + rot:r0 + LANE, :] = acc[r0 + rot:r0 + LANE].astype(o_ref.dtype)

    @pl.when(j < n_idx_tiles)
    def _():
        rope_rows(cp_ref, sp_ref, IDX_ROPE_DIM)

    @pl.when((j >= n_idx_tiles) & (j < n_idx_tiles + n_rope_tiles))
    def _():
        rope_rows(cf_ref, sf_ref, HEAD_DIM, q_mult)

    @pl.when(j >= n_idx_tiles + n_rope_tiles)
    def _():
        o_ref[...] = acc.astype(o_ref.dtype)


def _proj_t(wt, h, cos_f, sin_f, cos_p, sin_p, n_idx_rows, n_rope_rows, q_mult):
    n, d = wt.shape
    s = h.shape[0]
    tm, tn = _tile(s, TM_MM), _tile(n_rope_rows, TN_MM)
    assert n % tn == 0 and n_idx_rows % tn == 0 and n_rope_rows % tn == 0
    kern = functools.partial(_proj_t_kernel, n_idx_tiles=n_idx_rows // tn, n_rope_tiles=n_rope_rows // tn, tn=tn,
                             q_mult=q_mult)
    tab = lambda t: pl.BlockSpec((t.shape[0], tm), lambda i, j: (0, i))
    return pl.pallas_call(
        kern,
        grid=(s // tm, n // tn),
        in_specs=[pl.BlockSpec((tn, d), lambda i, j: (j, 0)),
                  pl.BlockSpec((tm, d), lambda i, j: (i, 0)),
                  tab(cos_f), tab(sin_f), tab(cos_p), tab(sin_p)],
        out_specs=pl.BlockSpec((tn, tm), lambda i, j: (j, i)),
        out_shape=jax.ShapeDtypeStruct((n, s), bf16),
        compiler_params=_cp(("arbitrary", "arbitrary")),
        name="proj_t",
    )(wt, h, cos_f, sin_f, cos_p, sin_p)


def _proj_k_kernel(h_ref, w_ref, c_ref, s_ref, o_ref, *, tn):
    acc = jnp.dot(h_ref[...], w_ref[...], preferred_element_type=f32)
    cos, sin = c_ref[...], s_ref[...]
    for hd in range(tn // LANE):
        x = acc[:, hd * LANE:(hd + 1) * LANE]
        y = x * cos + pltpu.roll(x, HEAD_DIM // 2, 1) * sin
        o_ref[:, hd * LANE:(hd + 1) * LANE] = y.astype(o_ref.dtype)


def _proj_k(h, w, cos_tok, sin_tok):
    s, d = h.shape
    n = w.shape[1]
    tm, tn = _tile(s, TM_MM), _tile(n, TN_MM)
    tab = pl.BlockSpec((tm, LANE), lambda i, j: (i, 0))
    return pl.pallas_call(
        functools.partial(_proj_k_kernel, tn=tn),
        grid=(s // tm, n // tn),
        in_specs=[pl.BlockSpec((tm, d), lambda i, j: (i, 0)),
                  pl.BlockSpec((d, tn), lambda i, j: (0, j)), tab, tab],
        out_specs=pl.BlockSpec((tm, tn), lambda i, j: (i, j)),
        out_shape=jax.ShapeDtypeStruct((s, n), bf16),
        compiler_params=_cp(("arbitrary", "arbitrary")),
        name="proj_k",
    )(h, w, cos_tok, sin_tok)


def _kw_kernel(h_ref, w_ref, g_ref, b_ref, c_ref, s1_ref, s2_ref, kidx_ref, widx_ref, *, w_scale):
    acc = jnp.dot(h_ref[...], w_ref[...], preferred_element_type=f32)
    k = acc[:, :LANE]
    mu = jnp.mean(k, axis=-1, keepdims=True)
    kc = k - mu
    var = jnp.mean(kc * kc, axis=-1, keepdims=True)
    y = kc * lax.rsqrt(var + EPS) * g_ref[...] + b_ref[...]
    half = IDX_ROPE_DIM // 2
    y = y * c_ref[...] + pltpu.roll(y, half, 1) * s1_ref[...] + pltpu.roll(y, LANE - half, 1) * s2_ref[...]
    kidx_ref[...] = y.astype(kidx_ref.dtype)
    widx_ref[...] = (acc[:, LANE:] * w_scale).T


def _kw(h, w_kw, ln_g, ln_b, c_tok, s1_tok, s2_tok, w_scale):
    s, d = h.shape
    tm = _tile(s, TM_NORM)
    row = pl.BlockSpec((1, LANE), lambda i: (0, 0))
    tab = pl.BlockSpec((tm, LANE), lambda i: (i, 0))
    return pl.pallas_call(
        functools.partial(_kw_kernel, w_scale=w_scale),
        grid=(s // tm,),
        in_specs=[pl.BlockSpec((tm, d), lambda i: (i, 0)),
                  pl.BlockSpec((d, 2 * LANE), lambda i: (0, 0)), row, row, tab, tab, tab],
        out_specs=[pl.BlockSpec((tm, LANE), lambda i: (i, 0)),
                   pl.BlockSpec((LANE, tm), lambda i: (0, i))],
        out_shape=[jax.ShapeDtypeStruct((s, LANE), bf16), jax.ShapeDtypeStruct((LANE, s), f32)],
        compiler_params=_cp(("arbitrary",)),
        name="kw",
    )(h, w_kw, ln_g, ln_b, c_tok, s1_tok, s2_tok)


def _int_key(x):
    bits = pltpu.bitcast(x, i32)
    return bits ^ ((bits >> 31) & 0x7FFFFFFF)


def _key_value(k):
    return pltpu.bitcast(k ^ ((k >> 31) & 0x7FFFFFFF), f32)


def _dsa_kernel(qidx_ref, q_ref, k_ref, v_ref, kidx_ref, w_ref, o_ref,
                key_ref, thr_ref, m_ref, l_ref, acc_ref, gm_ref, *, tq, tk, n_heads, n_idx_heads, k_sel, n_split,
                n_grp):
    i = pl.program_id(0)
    j = pl.program_id(1)
    last = ((i + 1) * tq - 1) // tk
    nkb = last + 1
    neg_key = int(np.array(-np.inf, np.float32).view(np.int32)) ^ 0x7FFFFFFF

    @pl.when(j == 0)
    def _score_and_select():
        m_ref[...] = jnp.full(m_ref.shape, NEG_BIG, f32)
        l_ref[...] = jnp.zeros(l_ref.shape, f32)
        acc_ref[...] = jnp.zeros(acc_ref.shape, f32)
        qpos = i * tq + lax.broadcasted_iota(i32, (tk, tq), 1)

        def score_body(c, carry):
            kc = kidx_ref[pl.ds(pl.multiple_of(c * tk, tk), tk), :]
            acc = jnp.zeros((tk, tq), f32)
            for h in range(n_idx_heads):
                lg = jnp.dot(kc, qidx_ref[h * LANE:(h + 1) * LANE, :], preferred_element_type=f32)
                acc = acc + w_ref[h:h + 1, :] * jnp.maximum(lg, 0.0)
            kpos = c * tk + lax.broadcasted_iota(i32, (tk, tq), 0)
            acc = jnp.where(kpos <= qpos, acc, -jnp.inf)
            key_ref[c] = _int_key(acc)
            gm = gm_ref[...]
            for r in range(tk // n_grp):
                gm = jnp.maximum(gm, acc[r * n_grp:(r + 1) * n_grp])
            gm_ref[...] = gm
            return carry

        gm_ref[...] = jnp.full(gm_ref.shape, -jnp.inf, f32)
        lax.fori_loop(0, nkb, score_body, 0)

        def count_ge(mid):
            def cnt_body(c, cnt):
                parts = [cnt, jnp.zeros_like(cnt), jnp.zeros_like(cnt), jnp.zeros_like(cnt)]
                for g in range(tk // SUBLANE):
                    blk = key_ref[c, g * SUBLANE:(g + 1) * SUBLANE, :]
                    parts[g % 4] = parts[g % 4] + jnp.where(blk >= mid, 1.0, 0.0)
                return (parts[0] + parts[1]) + (parts[2] + parts[3])

            cnt = lax.fori_loop(0, nkb, cnt_body, jnp.zeros((SUBLANE, tq), f32))
            return jnp.broadcast_to(jnp.sum(cnt, axis=0, keepdims=True), (SUBLANE, tq))

        def bis_body(carry):
            it, lo, hi, _ = carry
            mid = (lo >> 1) + (hi >> 1) + (lo & hi & 1)
            mid_v = _int_key(0.5 * _key_value(lo) + 0.5 * _key_value(hi))
            mid = jnp.where((it < N_VALUE_PASSES) & ((lo ^ hi) < 0) & (mid_v > lo) & (mid_v < hi), mid_v, mid)
            tot = count_ge(mid)
            ge = tot >= float(k_sel)
            lo = jnp.where(ge, mid, lo)
            hi = jnp.where(tot == float(k_sel), mid + 1, jnp.where(ge, hi, mid))
            gap = hi - lo
            still_open = jnp.max(jnp.where((gap > 1) | (gap < 0), 1.0, 0.0)) > 0.0
            return it + 1, lo, hi, still_open

        gm = gm_ref[...]
        lo0 = jnp.broadcast_to(_int_key(jnp.min(gm, axis=0, keepdims=True)), (SUBLANE, tq)) - 1
        hi0 = jnp.broadcast_to(_int_key(jnp.max(gm, axis=0, keepdims=True)), (SUBLANE, tq)) + 2
        _, lo, _, _ = lax.while_loop(lambda c: c[3] & (c[0] < MAX_BISECT), bis_body, (jnp.int32(0), lo0, hi0, True))
        thr_ref[...] = jnp.maximum(lo, neg_key + 1)

    @pl.when(j <= last)
    def _attend():
        parts = [slice(r * (tk // n_split), (r + 1) * (tk // n_split)) for r in range(n_split)]
        bias = [jnp.where(key_ref[j, ks, :] >= thr_ref[0:1, :], 0.0, NEG_BIG) for ks in parts]
        for h in range(n_heads):
            hs = slice(h * HEAD_DIM, (h + 1) * HEAD_DIM)
            q = q_ref[hs, :]
            sts = [jnp.dot(k_ref[ks, hs], q, preferred_element_type=f32) + bs for ks, bs in zip(parts, bias)]
            m_blk = functools.reduce(jnp.maximum, [jnp.max(st, axis=0, keepdims=True) for st in sts])
            m_old = m_ref[h]
            m_new = jnp.maximum(m_old, m_blk)
            alpha = jnp.exp2(m_old - m_new)
            ps = [jnp.exp2(st - m_new[0:1, :]) for st in sts]
            l_ref[h] = alpha * l_ref[h] + sum(jnp.sum(p, axis=0, keepdims=True) for p in ps)
            pv = sum(jnp.dot(v_ref[hs, ks], p.astype(bf16), preferred_element_type=f32)
                     for ks, p in zip(parts, ps))
            acc_ref[hs, :] = alpha[0:1, :] * acc_ref[hs, :] + pv
            m_ref[h] = m_new

    @pl.when(j == last)
    def _finish():
        for h in range(n_heads):
            hs = slice(h * HEAD_DIM, (h + 1) * HEAD_DIM)
            out_t = acc_ref[hs, :] / l_ref[h][0:1, :]
            o_ref[:, hs] = out_t.T.astype(o_ref.dtype)


def _dsa(feat_t, k_tok, kidx, widx_t, n_idx_rows, attn_w, k_sel):
    s = k_tok.shape[0]
    tq, tk = _tile(s, TQ_DSA), _tile(s, TK_DSA)
    n_grp = -(-k_sel // SUBLANE) * SUBLANE
    assert tk % n_grp == 0 and n_idx_rows % attn_w == 0 and tk % tq == 0
    nq, nk = s // tq, s // tk
    qb = n_idx_rows // attn_w
    last = lambda i: ((i + 1) * tq - 1) // tk
    kern = functools.partial(_dsa_kernel, tq=tq, tk=tk, n_heads=attn_w // HEAD_DIM,
                             n_idx_heads=n_idx_rows // IDX_HEAD_DIM, k_sel=k_sel, n_split=8, n_grp=n_grp)
    return pl.pallas_call(
        kern,
        grid=(nq, nk),
        in_specs=[pl.BlockSpec((n_idx_rows, tq), lambda i, j: (0, i)),
                  pl.BlockSpec((attn_w, tq), lambda i, j: (qb, i)),
                  pl.BlockSpec((tk, attn_w), lambda i, j: (jnp.minimum(j, last(i)), 0)),
                  pl.BlockSpec((attn_w, tk), lambda i, j: (qb + 1, jnp.minimum(j, last(i)))),
                  pl.BlockSpec((s, LANE), lambda i, j: (0, 0)),
                  pl.BlockSpec((LANE, tq), lambda i, j: (0, i))],
        out_specs=pl.BlockSpec((tq, attn_w), lambda i, j: (i, 0)),
        out_shape=jax.ShapeDtypeStruct((s, attn_w), bf16),
        scratch_shapes=[pltpu.VMEM((nk, tk, tq), i32),
                        pltpu.VMEM((SUBLANE, tq), i32),
                        pltpu.VMEM((attn_w // HEAD_DIM, SUBLANE, tq), f32),
                        pltpu.VMEM((attn_w // HEAD_DIM, SUBLANE, tq), f32),
                        pltpu.VMEM((attn_w, tq), f32),
                        pltpu.VMEM((n_grp, tq), f32)],
        compiler_params=_cp(("arbitrary", "arbitrary")),
        name="dsa",
    )(feat_t, feat_t, k_tok, feat_t, kidx, widx_t)


def _merge_kernel(h_ref, a_ref, b_ref, wg0_ref, wg1_ref, wo0_ref, wo1_ref, x_ref, g_ref, o_ref):
    h = h_ref[...]
    g0 = jax.nn.sigmoid(jnp.dot(h, wg0_ref[...], preferred_element_type=f32))
    g1 = jax.nn.sigmoid(jnp.dot(h, wg1_ref[...], preferred_element_type=f32))
    ya = jnp.dot(a_ref[...], wo0_ref[...], preferred_element_type=f32)
    yb = jnp.dot(b_ref[...], wo1_ref[...], preferred_element_type=f32)
    o_ref[...] = x_ref[...] + g_ref[...] * (g0 * ya + g1 * yb)


def _merge(h, a_out, b_out, w_gate, w_out, x, gate1):
    s, d = x.shape
    cw = a_out.shape[1]
    assert b_out.shape[1] == cw
    tm, tn = _tile(s, TM_MERGE), _tile(d, TN_MERGE)
    nj = d // tn
    return pl.pallas_call(
        _merge_kernel,
        grid=(s // tm, nj),
        in_specs=[pl.BlockSpec((tm, d), lambda i, j: (i, 0)),
                  pl.BlockSpec((tm, cw), lambda i, j: (i, 0)),
                  pl.BlockSpec((tm, cw), lambda i, j: (i, 0)),
                  pl.BlockSpec((d, tn), lambda i, j: (0, j)),
                  pl.BlockSpec((d, tn), lambda i, j: (0, j + nj)),
                  pl.BlockSpec((cw, tn), lambda i, j: (0, j)),
                  pl.BlockSpec((cw, tn), lambda i, j: (1, j)),
                  pl.BlockSpec((tm, tn), lambda i, j: (i, j)),
                  pl.BlockSpec((1, tn), lambda i, j: (0, j))],
        out_specs=pl.BlockSpec((tm, tn), lambda i, j: (i, j)),
        out_shape=jax.ShapeDtypeStruct((s, d), f32),
        compiler_params=_cp(("arbitrary", "arbitrary")),
        name="merge",
    )(h, a_out, b_out, w_gate, w_gate, w_out, w_out, x, gate1)


def _top_sorted(x, k):
    n = x.shape[0]
    rows = lax.broadcasted_iota(i32, x.shape, 0).astype(f32)
    cur, tops = x, []
    for _ in range(k):
        m = jnp.max(cur, axis=0, keepdims=True)
        first = jnp.min(jnp.where(cur == m, rows, float(n)), axis=0, keepdims=True)
        cur = jnp.where(rows == first, -jnp.inf, cur)
        tops.append(m)
    return tops


def _route_kernel(h_ref, wq_ref, sk_ref, a_ref, b_ref, ea_ref, eb_ref, thr_ref, *, half, topk):
    qt = jnp.dot(wq_ref[...], h_ref[...], preferred_element_type=f32).astype(bf16)
    a = jnp.dot(sk_ref[0, 0], qt[:half], preferred_element_type=f32)
    b = jnp.dot(sk_ref[0, 1], qt[half:], preferred_element_type=f32)
    ta = _top_sorted(a, topk)
    tb = jnp.concatenate(_top_sorted(b, topk), axis=0)
    n_q = lambda p: -(-(topk // (p + 1)) // SUBLANE) * SUBLANE
    cand = jnp.concatenate([t + tb[:n_q(p)] for p, t in enumerate(ta)], axis=0)
    best = _top_sorted(cand, topk)
    thr = best[-1]
    z = jnp.ones_like(thr)
    for t in best[1:]:
        z = z + jnp.exp(t - best[0])
    a_ref[0] = a
    b_ref[0] = b
    ea_ref[0] = jnp.exp(a - ta[0])
    eb_ref[0] = jnp.exp(b - tb[0:1]) / z
    thr_ref[0] = jnp.broadcast_to(thr, thr_ref.shape[1:])


def _route(h2t, wq_t, sub_keys):
    d, s = h2t.shape
    heads, _, n_keys, half = sub_keys.shape
    tt = _tile(s, TT_ROUTE)
    big = pl.BlockSpec((1, n_keys, tt), lambda t, h: (h, 0, t))
    shp = jax.ShapeDtypeStruct((heads, n_keys, s), f32)
    return pl.pallas_call(
        functools.partial(_route_kernel, half=half, topk=PEER_TOPK),
        grid=(s // tt, heads),
        in_specs=[pl.BlockSpec((d, tt), lambda t, h: (0, t)),
                  pl.BlockSpec((2 * half, d), lambda t, h: (h, 0)),
                  pl.BlockSpec((1, 2, n_keys, half), lambda t, h: (h, 0, 0, 0))],
        out_specs=[big, big, big, big, pl.BlockSpec((1, SUBLANE, tt), lambda t, h: (h, 0, t))],
        out_shape=[shp, shp, shp, shp, jax.ShapeDtypeStruct((heads, SUBLANE, s), f32)],
        compiler_params=_cp(("arbitrary", "arbitrary")),
        name="route",
    )(h2t, wq_t, sub_keys)


def _peer_kernel(h_ref, u_ref, v_ref, a_ref, b_ref, ea_ref, eb_ref, thr_ref, o_ref, at_ref, g_ref, wt_ref,
                 *, ec, sub, heads, n_keys, lw):
    e = pl.program_id(1)
    d, tt = h_ref.shape
    nsub = ec // sub
    pieces = [(ii, lh) for ii in range(sub // n_keys) for lh in range(tt // lw)]
    dn = d // len(pieces)

    @pl.when(e == 0)
    def _():
        o_ref[...] = jnp.zeros(o_ref.shape, f32)

    def gate_piece(c, ii, lh):
        row = e * (ec // n_keys) + c * (sub // n_keys) + ii
        rs = slice(c * sub + ii * n_keys, c * sub + (ii + 1) * n_keys)
        ls = slice(lh * lw, (lh + 1) * lw)
        g = jnp.zeros((n_keys, lw), f32)
        for hd in range(heads):
            val = a_ref[hd, pl.ds(row, 1), :][:, ls] + b_ref[hd, :, ls]
            g = g + jnp.where(val >= thr_ref[hd, 0:1, ls],
                              ea_ref[hd, pl.ds(row, 1), :][:, ls] * eb_ref[hd, :, ls], 0.0)
        g_ref[rs, ls] = g

    def combine(c):
        for ii, lh in pieces:
            rs = slice(c * sub + ii * n_keys, c * sub + (ii + 1) * n_keys)
            ls = slice(lh * lw, (lh + 1) * lw)
            at = at_ref[rs, ls]
            act = 0.5 * at * (1.0 + lax.erf(at * np.float32(np.sqrt(0.5))))
            wt_ref[ls, rs] = (g_ref[rs, ls] * act).T.astype(bf16)

    def out_chunk(c, k):
        cs = slice(c * sub, (c + 1) * sub)
        ns = slice(k * dn, (k + 1) * dn)
        o_ref[:, ns] += jnp.dot(wt_ref[:, cs], v_ref[cs, ns], preferred_element_type=f32)

    h = h_ref[...]
    for p in pieces:
        gate_piece(0, *p)
    for c in range(nsub):
        cs = slice(c * sub, (c + 1) * sub)
        at_ref[cs, :] = jnp.dot(u_ref[cs, :], h, preferred_element_type=f32)
    for c in range(nsub):
        combine(c)
        for k, p in enumerate(pieces):
            out_chunk(c, k)
            if c + 1 < nsub:
                gate_piece(c + 1, *p)


def _peer(h2t, u, v, a, b, ea, eb, thr):
    d, s = h2t.shape
    n_exp = u.shape[0]
    heads, n_keys, _ = a.shape
    tt, ec = _tile(s, TT_PEER), _tile(n_exp, EC_PEER)
    sub = min(ec, SUB_PEER)
    assert ec % sub == 0 and sub % n_keys == 0
    once = pl.Buffered(1)
    tab = pl.BlockSpec((heads, n_keys, tt), lambda t, e: (0, 0, t), pipeline_mode=once)
    return pl.pallas_call(
        functools.partial(_peer_kernel, ec=ec, sub=sub, heads=heads, n_keys=n_keys, lw=LANE),
        grid=(s // tt, n_exp // ec),
        in_specs=[pl.BlockSpec((d, tt), lambda t, e: (0, t), pipeline_mode=once),
                  pl.BlockSpec((ec, d), lambda t, e: (e, 0)),
                  pl.BlockSpec((ec, d), lambda t, e: (e, 0)),
                  tab, tab, tab, tab,
                  pl.BlockSpec((heads, SUBLANE, tt), lambda t, e: (0, 0, t), pipeline_mode=once)],
        out_specs=pl.BlockSpec((tt, d), lambda t, e: (t, 0)),
        out_shape=jax.ShapeDtypeStruct((s, d), f32),
        scratch_shapes=[pltpu.VMEM((ec, tt), f32), pltpu.VMEM((ec, tt), f32), pltpu.VMEM((tt, ec), bf16)],
        compiler_params=_cp(("arbitrary", "arbitrary")),
        name="peer",
    )(h2t, u, v, a, b, ea, eb, thr)


def _final_kernel(x_ref, p_ref, g2_ref, fg_ref, o_ref):
    x = x_ref[...] + g2_ref[...] * p_ref[...]
    o_ref[...] = x * lax.rsqrt(jnp.mean(x * x, axis=-1, keepdims=True) + EPS) * fg_ref[...]


def _final(x1, p, gate2, fg):
    s, d = x1.shape
    tm = _tile(s, TM_FINAL)
    row = pl.BlockSpec((1, d), lambda i: (0, 0))
    blk = pl.BlockSpec((tm, d), lambda i: (i, 0))
    return pl.pallas_call(
        _final_kernel,
        grid=(s // tm,),
        in_specs=[blk, blk, row, row],
        out_specs=blk,
        out_shape=jax.ShapeDtypeStruct((s, d), f32),
        compiler_params=_cp(("arbitrary",)),
        name="final",
    )(x1, p, gate2, fg)


def _rope_tables(positions, dim):
    inv_freq = ROPE_THETA ** (-jnp.arange(0, dim, 2, dtype=f32) / dim)
    ang = positions.astype(f32)[:, None] * inv_freq
    return jnp.cos(ang), jnp.sin(ang)


def kernel(x, c, positions, w_ada, b_ada, norm_mix_g, norm_ffn_g, w_in, conv_w, conv_b, conv_ln_g, conv_ln_b,
           k_idx_ln_g, k_idx_ln_b, w_out, peer_w_q, peer_sub_keys, peer_u, peer_v, final_norm_g):
    bsz, s, d = x.shape
    depth = w_ada.shape[0]
    assert bsz == 1 and depth == 1, "one sequence, one layer"
    cw = conv_w.shape[-1]
    aw = N_ATTN_HEADS * HEAD_DIM
    iw = N_IDX_HEADS * IDX_HEAD_DIM
    assert cw == aw
    x2d = x.reshape(s, d)
    pos = positions.reshape(s)

    w_in2 = w_in.reshape(d, -1)
    o_q, o_k, o_v, o_qi = 2 * cw, 2 * cw + aw, 2 * cw + 2 * aw, 2 * cw + 3 * aw
    o_ki = o_qi + iw
    o_wi = o_ki + IDX_HEAD_DIM
    o_g = o_wi + N_IDX_HEADS
    w_glu = w_in2[:, :o_q].astype(bf16)
    w_feat_t = jnp.concatenate([w_in2[:, o_qi:o_ki].astype(bf16), w_in2[:, o_q:o_k].astype(bf16),
                                w_in2[:, o_v:o_qi].astype(bf16)], axis=1).T
    w_k = w_in2[:, o_k:o_v].astype(bf16)
    w_kw = w_in2[:, o_ki:o_ki + 2 * LANE].astype(bf16)
    w_gate = w_in2[:, o_g:].astype(bf16)
    w_out2 = w_out.reshape(cw + aw, d).astype(bf16)
    wq_t = peer_w_q.reshape(d, -1).astype(bf16).T
    sub_keys = peer_sub_keys.reshape(peer_sub_keys.shape[1:]).astype(bf16)
    u_exp = peer_u.reshape(-1, d).astype(bf16)
    v_exp = peer_v.reshape(-1, d).astype(bf16)

    cos_f, sin_f = _rope_tables(pos, HEAD_DIM)
    cos_p, sin_p = _rope_tables(pos, IDX_ROPE_DIM)
    cos_tok = jnp.concatenate([cos_f, cos_f], axis=1)
    sin_tok = jnp.concatenate([-sin_f, sin_f], axis=1)
    zeros_p = jnp.zeros_like(sin_p)
    pad = jnp.zeros((s, LANE - IDX_ROPE_DIM), f32)
    c_tok = jnp.concatenate([cos_p, cos_p, pad + 1.0], axis=1)
    s1_tok = jnp.concatenate([zeros_p, sin_p, pad], axis=1)
    s2_tok = jnp.concatenate([-sin_p, zeros_p, pad], axis=1)

    ada = _ada(jnp.broadcast_to(c.reshape(1, d), (SUBLANE, d)), w_ada.reshape(d, -1), b_ada.reshape(1, -1))[0:1]
    shift1, scale1, gate1, shift2, scale2, gate2 = jnp.split(ada, 6, axis=-1)

    h = _norm_mod(x2d, norm_mix_g.reshape(1, d), scale1, shift1)
    u = _glu(h, w_glu, cw)
    a_out = _conv(u, conv_w.reshape(-1, cw), conv_b.reshape(1, cw), conv_ln_g.reshape(1, cw), conv_ln_b.reshape(1, cw))
    feat_t = _proj_t(w_feat_t, h, cos_f.T, sin_f.T, cos_p.T, sin_p.T, iw, aw, HEAD_DIM ** -0.5 * LOG2E)
    k_tok = _proj_k(h, w_k, cos_tok, sin_tok)
    w_scale = float(N_IDX_HEADS ** -0.5 * IDX_HEAD_DIM ** -0.5)
    kidx, widx_t = _kw(h, w_kw, k_idx_ln_g.reshape(1, -1), k_idx_ln_b.reshape(1, -1), c_tok, s1_tok, s2_tok, w_scale)
    b_out = _dsa(feat_t, k_tok, kidx, widx_t, iw, aw, min(TOPK_MAX, s // 4))
    x1 = _merge(h, a_out, b_out, w_gate, w_out2, x2d, gate1)

    h2t = _norm_mod(x1, norm_ffn_g.reshape(1, d), scale2, shift2, transposed=True)
    a, b, ea, eb, thr = _route(h2t, wq_t, sub_keys)
    p = _peer(h2t, u_exp, v_exp, a, b, ea, eb, thr)
    return _final(x1, p, gate2, final_norm_g.reshape(1, d)).reshape(bsz, s, d)
```

```python
import functools

import numpy as np
import jax
import jax.numpy as jnp
from jax import lax
from jax.experimental import pallas as pl
from jax.experimental.pallas import tpu as pltpu

N_ATTN_HEADS = 16
HEAD_DIM = 128
N_IDX_HEADS = 32
IDX_HEAD_DIM = 128
IDX_ROPE_DIM = 64
TOPK_MAX = 256
ROPE_THETA = 10000.0
PEER_TOPK = 16
EPS = 1e-6

LANE = 128
SUBLANE = 8
VMEM_LIMIT = 56 * 1024 * 1024
NEG_BIG = -1e30
LOG2E = 1.4426950408889634
INT_MIN = -2 ** 31
INT_MAX = 2 ** 31 - 1
N_VALUE_PASSES = 12
MAX_BISECT = N_VALUE_PASSES + 33

TM_NORM = 512
TM_MM = 1024
TN_MM = 512
TM_CONV = 256
TQ_DSA = 256
TK_DSA = 1024
TM_MERGE = 512
TN_MERGE = 512
TT_ROUTE = 512
TT_PEER = 512
EC_PEER = 512
SUB_PEER = 256
TM_FINAL = 256

f32 = jnp.float32
bf16 = jnp.bfloat16
i32 = jnp.int32

_NT = (((1,), (1,)), ((), ()))
_TN = (((0,), (0,)), ((), ()))


def _cp(sem, vmem=VMEM_LIMIT):
    return pltpu.CompilerParams(dimension_semantics=sem, vmem_limit_bytes=vmem)


def _tile(n, t):
    t = min(n, t)
    assert n % t == 0, (n, t)
    return t


def _ada_kernel(c_ref, w_ref, b_ref, o_ref):
    c = c_ref[...]
    ca = (c * jax.nn.sigmoid(c)).astype(bf16)
    o_ref[...] = jnp.dot(ca, w_ref[...].astype(bf16), preferred_element_type=f32) + b_ref[...]


def _ada(c8, w, b):
    d, n = w.shape
    tn = _tile(n, 512)
    return pl.pallas_call(
        _ada_kernel,
        grid=(n // tn,),
        in_specs=[pl.BlockSpec((SUBLANE, d), lambda j: (0, 0)),
                  pl.BlockSpec((d, tn), lambda j: (0, j)),
                  pl.BlockSpec((1, tn), lambda j: (0, j))],
        out_specs=pl.BlockSpec((SUBLANE, tn), lambda j: (0, j)),
        out_shape=jax.ShapeDtypeStruct((SUBLANE, n), f32),
        compiler_params=_cp(("arbitrary",)),
        name="ada",
    )(c8, w, b)


def _norm_mod_kernel(x_ref, g_ref, sc_ref, sh_ref, o_ref):
    x = x_ref[...]
    y = x * lax.rsqrt(jnp.mean(x * x, axis=-1, keepdims=True) + EPS) * g_ref[...]
    o_ref[...] = (y * (1.0 + sc_ref[...]) + sh_ref[...]).astype(o_ref.dtype)


def _norm_mod_t_kernel(x_ref, g_ref, sc_ref, sh_ref, o_ref):
    x = x_ref[...]
    y = x * lax.rsqrt(jnp.mean(x * x, axis=-1, keepdims=True) + EPS) * g_ref[...]
    o_ref[...] = (y * (1.0 + sc_ref[...]) + sh_ref[...]).T.astype(o_ref.dtype)


def _norm_mod(x, g, scale, shift, transposed=False):
    s, d = x.shape
    tm = _tile(s, TM_NORM)
    row = pl.BlockSpec((1, d), lambda i: (0, 0))
    return pl.pallas_call(
        _norm_mod_t_kernel if transposed else _norm_mod_kernel,
        grid=(s // tm,),
        in_specs=[pl.BlockSpec((tm, d), lambda i: (i, 0)), row, row, row],
        out_specs=pl.BlockSpec((d, tm), lambda i: (0, i)) if transposed else pl.BlockSpec((tm, d), lambda i: (i, 0)),
        out_shape=jax.ShapeDtypeStruct((d, s) if transposed else (s, d), bf16),
        compiler_params=_cp(("arbitrary",)),
        name="norm_mod_t" if transposed else "norm_mod",
    )(x, g, scale, shift)


def _glu_kernel(h_ref, wa_ref, wg_ref, o_ref):
    h = h_ref[...]
    a = jnp.dot(h, wa_ref[...], preferred_element_type=f32)
    g = jnp.dot(h, wg_ref[...], preferred_element_type=f32)
    o_ref[...] = a * jax.nn.sigmoid(g)


def _glu(h, w_glu, cw):
    s, d = h.shape
    tm, tn = _tile(s, TM_MM), _tile(cw, TN_MM)
    nj = cw // tn
    return pl.pallas_call(
        _glu_kernel,
        grid=(s // tm, nj),
        in_specs=[pl.BlockSpec((tm, d), lambda i, j: (i, 0)),
                  pl.BlockSpec((d, tn), lambda i, j: (0, j)),
                  pl.BlockSpec((d, tn), lambda i, j: (0, j + nj))],
        out_specs=pl.BlockSpec((tm, tn), lambda i, j: (i, j)),
        out_shape=jax.ShapeDtypeStruct((s, cw), f32),
        compiler_params=_cp(("arbitrary", "arbitrary")),
        name="glu",
    )(h, w_glu, w_glu)


def _conv_kernel(prev_ref, cur_ref, w_ref, cb_ref, g_ref, b_ref, o_ref, ubuf, ybuf, *, tm, kw, halo, rs):
    i = pl.program_id(0)
    cw = cur_ref.shape[1]
    ubuf[0:halo, :] = jnp.where(i > 0, prev_ref[...], 0.0)
    ubuf[halo:, :] = cur_ref[...]
    off = halo - (kw - 1)

    def row_body(r, carry):
        r0 = pl.multiple_of(r * rs, rs)
        for c in range(cw // LANE):
            cs = slice(c * LANE, (c + 1) * LANE)
            acc = jnp.zeros((rs, LANE), f32) + cb_ref[:, cs]
            win = ubuf[pl.ds(r0, rs + halo), cs]
            for r in range(SUBLANE):
                taps = [k for k in range(kw) if (off + k) % SUBLANE == r]
                if taps:
                    sh = pltpu.roll(win, rs + halo - r, 0) if r else win
                    for k in taps:
                        a = off + k - r
                        acc = acc + w_ref[k:k + 1, cs] * sh[a:a + rs]
            ybuf[pl.ds(r0, rs), cs] = acc
        return carry

    lax.fori_loop(0, tm // rs, row_body, 0)
    y = ybuf[...]
    mu = jnp.mean(y, axis=-1, keepdims=True)
    yc = y - mu
    var = jnp.mean(yc * yc, axis=-1, keepdims=True)
    z = yc * lax.rsqrt(var + EPS) * g_ref[...] + b_ref[...]
    o_ref[...] = (z * jax.nn.sigmoid(z)).astype(o_ref.dtype)


def _conv(u, conv_w, conv_b, ln_g, ln_b):
    s, cw = u.shape
    kw = conv_w.shape[0]
    tm = _tile(s, TM_CONV)
    halo = -(-(kw - 1) // SUBLANE) * SUBLANE
    rs = 64
    assert tm % halo == 0 and tm % rs == 0
    kwp = -(-kw // SUBLANE) * SUBLANE
    w_pad = jnp.zeros((kwp, cw), f32).at[:kw].set(conv_w)
    row = pl.BlockSpec((1, cw), lambda i: (0, 0))
    return pl.pallas_call(
        functools.partial(_conv_kernel, tm=tm, kw=kw, halo=halo, rs=rs),
        grid=(s // tm,),
        in_specs=[pl.BlockSpec((halo, cw), lambda i: (jnp.maximum(i * (tm // halo) - 1, 0), 0)),
                  pl.BlockSpec((tm, cw), lambda i: (i, 0)),
                  pl.BlockSpec((kwp, cw), lambda i: (0, 0)), row, row, row],
        out_specs=pl.BlockSpec((tm, cw), lambda i: (i, 0)),
        out_shape=jax.ShapeDtypeStruct((s, cw), bf16),
        scratch_shapes=[pltpu.VMEM((halo + tm, cw), f32), pltpu.VMEM((tm, cw), f32)],
        compiler_params=_cp(("arbitrary",)),
        name="conv",
    )(u, u, w_pad, conv_b, ln_g, ln_b)


def _proj_t_kernel(w_ref, h_ref, cf_ref, sf_ref, cp_ref, sp_ref, o_ref, *, n_idx_tiles, n_rope_tiles, tn, q_mult):
    j = pl.program_id(1)
    acc = lax.dot_general(w_ref[...], h_ref[...], _NT, preferred_element_type=f32)

    def rope_rows(cos_ref, sin_ref, rot, mult=None):
        half = rot // 2
        cos, sin = cos_ref[...], sin_ref[...]
        if mult is not None:
            cos, sin = cos * mult, sin * mult
        for hd in range(tn // LANE):
            r0 = hd * LANE
            x1 = acc[r0:r0 + half]
            x2 = acc[r0 + half:r0 + rot]
            o_ref[r0:r0 + half, :] = (x1 * cos - x2 * sin).astype(o_ref.dtype)
            o_ref[r0 + half:r0 + rot, :] = (x2 * cos + x1 * sin).astype(o_ref.dtype)
            if rot < LANE:
                o_ref[r0 + rot:r0 + LANE, :] = acc[r0 + rot:r0 + LANE].astype(o_ref.dtype)

    @pl.when(j < n_idx_tiles)
    def _():
        rope_rows(cp_ref, sp_ref, IDX_ROPE_DIM)

    @pl.when((j >= n_idx_tiles) & (j < n_idx_tiles + n_rope_tiles))
    def _():
        rope_rows(cf_ref, sf_ref, HEAD_DIM, q_mult)

    @pl.when(j >= n_idx_tiles + n_rope_tiles)
    def _():
        o_ref[...] = acc.astype(o_ref.dtype)


def _proj_t(wt, h, cos_f, sin_f, cos_p, sin_p, n_idx_rows, n_rope_rows, q_mult):
    n, d = wt.shape
    s = h.shape[0]
    tm, tn = _tile(s, TM_MM), _tile(n_rope_rows, TN_MM)
    assert n % tn == 0 and n_idx_rows % tn == 0 and n_rope_rows % tn == 0
    kern = functools.partial(_proj_t_kernel, n_idx_tiles=n_idx_rows // tn, n_rope_tiles=n_rope_rows // tn, tn=tn,
                             q_mult=q_mult)
    tab = lambda t: pl.BlockSpec((t.shape[0], tm), lambda i, j: (0, i))
    return pl.pallas_call(
        kern,
        grid=(s // tm, n // tn),
        in_specs=[pl.BlockSpec((tn, d), lambda i, j: (j, 0)),
                  pl.BlockSpec((tm, d), lambda i, j: (i, 0)),
                  tab(cos_f), tab(sin_f), tab(cos_p), tab(sin_p)],
        out_specs=pl.BlockSpec((tn, tm), lambda i, j: (j, i)),
        out_shape=jax.ShapeDtypeStruct((n, s), bf16),
        compiler_params=_cp(("arbitrary", "arbitrary")),
        name="proj_t",
    )(wt, h, cos_f, sin_f, cos_p, sin_p)


def _proj_k_kernel(h_ref, w_ref, c_ref, s_ref, o_ref, *, tn):
    acc = jnp.dot(h_ref[...], w_ref[...], preferred_element_type=f32)
    cos, sin = c_ref[...], s_ref[...]
    for hd in range(tn // LANE):
        x = acc[:, hd * LANE:(hd + 1) * LANE]
        y = x * cos + pltpu.roll(x, HEAD_DIM // 2, 1) * sin
        o_ref[:, hd * LANE:(hd + 1) * LANE] = y.astype(o_ref.dtype)


def _proj_k(h, w, cos_tok, sin_tok):
    s, d = h.shape
    n = w.shape[1]
    tm, tn = _tile(s, TM_MM), _tile(n, TN_MM)
    tab = pl.BlockSpec((tm, LANE), lambda i, j: (i, 0))
    return pl.pallas_call(
        functools.partial(_proj_k_kernel, tn=tn),
        grid=(s // tm, n // tn),
        in_specs=[pl.BlockSpec((tm, d), lambda i, j: (i, 0)),
                  pl.BlockSpec((d, tn), lambda i, j: (0, j)), tab, tab],
        out_specs=pl.BlockSpec((tm, tn), lambda i, j: (i, j)),
        out_shape=jax.ShapeDtypeStruct((s, n), bf16),
        compiler_params=_cp(("arbitrary", "arbitrary")),
        name="proj_k",
    )(h, w, cos_tok, sin_tok)


def _kw_kernel(h_ref, w_ref, g_ref, b_ref, c_ref, s1_ref, s2_ref, kidx_ref, widx_ref, *, w_scale):
    acc = jnp.dot(h_ref[...], w_ref[...], preferred_element_type=f32)
    k = acc[:, :LANE]
    mu = jnp.mean(k, axis=-1, keepdims=True)
    kc = k - mu
    var = jnp.mean(kc * kc, axis=-1, keepdims=True)
    y = kc * lax.rsqrt(var + EPS) * g_ref[...] + b_ref[...]
    half = IDX_ROPE_DIM // 2
    y = y * c_ref[...] + pltpu.roll(y, half, 1) * s1_ref[...] + pltpu.roll(y, LANE - half, 1) * s2_ref[...]
    kidx_ref[...] = y.astype(kidx_ref.dtype)
    widx_ref[...] = (acc[:, LANE:] * w_scale).T


def _kw(h, w_kw, ln_g, ln_b, c_tok, s1_tok, s2_tok, w_scale):
    s, d = h.shape
    tm = _tile(s, TM_NORM)
    row = pl.BlockSpec((1, LANE), lambda i: (0, 0))
    tab = pl.BlockSpec((tm, LANE), lambda i: (i, 0))
    return pl.pallas_call(
        functools.partial(_kw_kernel, w_scale=w_scale),
        grid=(s // tm,),
        in_specs=[pl.BlockSpec((tm, d), lambda i: (i, 0)),
                  pl.BlockSpec((d, 2 * LANE), lambda i: (0, 0)), row, row, tab, tab, tab],
        out_specs=[pl.BlockSpec((tm, LANE), lambda i: (i, 0)),
                   pl.BlockSpec((LANE, tm), lambda i: (0, i))],
        out_shape=[jax.ShapeDtypeStruct((s, LANE), bf16), jax.ShapeDtypeStruct((LANE, s), f32)],
        compiler_params=_cp(("arbitrary",)),
        name="kw",
    )(h, w_kw, ln_g, ln_b, c_tok, s1_tok, s2_tok)


def _int_key(x):
    bits = pltpu.bitcast(x, i32)
    return bits ^ ((bits >> 31) & 0x7FFFFFFF)


def _key_value(k):
    return pltpu.bitcast(k ^ ((k >> 31) & 0x7FFFFFFF), f32)


def _dsa_kernel(qblk_ref, kblk_ref, qidx_ref, q_ref, k_ref, v_ref, kidx_ref, w_ref, o_ref,
                key_ref, thr_ref, m_ref, l_ref, acc_ref, gm_ref, *, tq, tk, n_heads, n_idx_heads, k_sel, n_split,
                n_grp):
    i = qblk_ref[pl.program_id(0)]
    j = kblk_ref[pl.program_id(0)]
    last = ((i + 1) * tq - 1) // tk
    nkb = last + 1
    neg_key = int(np.array(-np.inf, np.float32).view(np.int32)) ^ 0x7FFFFFFF

    @pl.when(j == 0)
    def _score_and_select():
        m_ref[...] = jnp.full(m_ref.shape, NEG_BIG, f32)
        l_ref[...] = jnp.zeros(l_ref.shape, f32)
        acc_ref[...] = jnp.zeros(acc_ref.shape, f32)
        qpos = i * tq + lax.broadcasted_iota(i32, (tk, tq), 1)

        def score_body(c, carry):
            kc = kidx_ref[pl.ds(pl.multiple_of(c * tk, tk), tk), :]
            acc = jnp.zeros((tk, tq), f32)
            for h in range(n_idx_heads):
                lg = jnp.dot(kc, qidx_ref[h * LANE:(h + 1) * LANE, :], preferred_element_type=f32)
                acc = acc + w_ref[h:h + 1, :] * jnp.maximum(lg, 0.0)
            kpos = c * tk + lax.broadcasted_iota(i32, (tk, tq), 0)
            acc = jnp.where(kpos <= qpos, acc, -jnp.inf)
            key_ref[c] = _int_key(acc)
            gm = gm_ref[...]
            for r in range(tk // n_grp):
                gm = jnp.maximum(gm, acc[r * n_grp:(r + 1) * n_grp])
            gm_ref[...] = gm
            return carry

        gm_ref[...] = jnp.full(gm_ref.shape, -jnp.inf, f32)
        lax.fori_loop(0, nkb, score_body, 0)

        def count_ge(mid):
            def cnt_body(c, cnt):
                parts = [cnt, jnp.zeros_like(cnt), jnp.zeros_like(cnt), jnp.zeros_like(cnt)]
                for g in range(tk // SUBLANE):
                    blk = key_ref[c, g * SUBLANE:(g + 1) * SUBLANE, :]
                    parts[g % 4] = parts[g % 4] + jnp.where(blk >= mid, 1.0, 0.0)
                return (parts[0] + parts[1]) + (parts[2] + parts[3])

            cnt = lax.fori_loop(0, nkb, cnt_body, jnp.zeros((SUBLANE, tq), f32))
            return jnp.broadcast_to(jnp.sum(cnt, axis=0, keepdims=True), (SUBLANE, tq))

        def bis_body(carry):
            it, lo, hi, _ = carry
            mid = (lo >> 1) + (hi >> 1) + (lo & hi & 1)
            mid_v = _int_key(0.5 * _key_value(lo) + 0.5 * _key_value(hi))
            mid = jnp.where((it < N_VALUE_PASSES) & ((lo ^ hi) < 0) & (mid_v > lo) & (mid_v < hi), mid_v, mid)
            tot = count_ge(mid)
            ge = tot >= float(k_sel)
            lo = jnp.where(ge, mid, lo)
            hi = jnp.where(tot == float(k_sel), mid + 1, jnp.where(ge, hi, mid))
            gap = hi - lo
            still_open = jnp.max(jnp.where((gap > 1) | (gap < 0), 1.0, 0.0)) > 0.0
            return it + 1, lo, hi, still_open

        gm = gm_ref[...]
        lo0 = jnp.broadcast_to(_int_key(jnp.min(gm, axis=0, keepdims=True)), (SUBLANE, tq)) - 1
        hi0 = jnp.broadcast_to(_int_key(jnp.max(gm, axis=0, keepdims=True)), (SUBLANE, tq)) + 2
        _, lo, _, _ = lax.while_loop(lambda c: c[3] & (c[0] < MAX_BISECT), bis_body, (jnp.int32(0), lo0, hi0, True))
        thr_ref[...] = jnp.maximum(lo, neg_key + 1)

    @pl.when(j <= last)
    def _attend():
        parts = [slice(r * (tk // n_split), (r + 1) * (tk // n_split)) for r in range(n_split)]
        bias = [jnp.where(key_ref[j, ks, :] >= thr_ref[0:1, :], 0.0, NEG_BIG) for ks in parts]
        for h in range(n_heads):
            hs = slice(h * HEAD_DIM, (h + 1) * HEAD_DIM)
            q = q_ref[hs, :]
            sts = [jnp.dot(k_ref[ks, hs], q, preferred_element_type=f32) + bs for ks, bs in zip(parts, bias)]
            m_blk = functools.reduce(jnp.maximum, [jnp.max(st, axis=0, keepdims=True) for st in sts])
            m_old = m_ref[h]
            m_new = jnp.maximum(m_old, m_blk)
            alpha = jnp.exp2(m_old - m_new)
            ps = [jnp.exp2(st - m_new[0:1, :]) for st in sts]
            l_ref[h] = alpha * l_ref[h] + sum(jnp.sum(p, axis=0, keepdims=True) for p in ps)
            pv = sum(jnp.dot(v_ref[hs, ks], p.astype(bf16), preferred_element_type=f32)
                     for ks, p in zip(parts, ps))
            acc_ref[hs, :] = alpha[0:1, :] * acc_ref[hs, :] + pv
            m_ref[h] = m_new

    @pl.when(j == last)
    def _finish():
        for h in range(n_heads):
            hs = slice(h * HEAD_DIM, (h + 1) * HEAD_DIM)
            out_t = acc_ref[hs, :] / l_ref[h][0:1, :]
            o_ref[:, hs] = out_t.T.astype(o_ref.dtype)


def _dsa(feat_t, k_tok, kidx, widx_t, n_idx_rows, attn_w, k_sel):
    s = k_tok.shape[0]
    tq, tk = _tile(s, TQ_DSA), _tile(s, TK_DSA)
    n_grp = -(-k_sel // SUBLANE) * SUBLANE
    assert tk % n_grp == 0 and n_idx_rows % attn_w == 0 and tk % tq == 0
    nq, nk = s // tq, s // tk
    qb = n_idx_rows // attn_w
    pairs = [(i, j) for i in range(nq) for j in range(((i + 1) * tq - 1) // tk + 1)]
    qblk = jnp.asarray([p[0] for p in pairs], i32)
    kblk = jnp.asarray([p[1] for p in pairs], i32)
    kern = functools.partial(_dsa_kernel, tq=tq, tk=tk, n_heads=attn_w // HEAD_DIM,
                             n_idx_heads=n_idx_rows // IDX_HEAD_DIM, k_sel=k_sel, n_split=8, n_grp=n_grp)
    grid_spec = pltpu.PrefetchScalarGridSpec(
        num_scalar_prefetch=2,
        grid=(len(pairs),),
        in_specs=[pl.BlockSpec((n_idx_rows, tq), lambda t, qi, ki: (0, qi[t])),
                  pl.BlockSpec((attn_w, tq), lambda t, qi, ki: (qb, qi[t])),
                  pl.BlockSpec((tk, attn_w), lambda t, qi, ki: (ki[t], 0)),
                  pl.BlockSpec((attn_w, tk), lambda t, qi, ki: (qb + 1, ki[t])),
                  pl.BlockSpec((s, LANE), lambda t, qi, ki: (0, 0)),
                  pl.BlockSpec((LANE, tq), lambda t, qi, ki: (0, qi[t]))],
        out_specs=pl.BlockSpec((tq, attn_w), lambda t, qi, ki: (qi[t], 0)),
        scratch_shapes=[pltpu.VMEM((nk, tk, tq), i32),
                        pltpu.VMEM((SUBLANE, tq), i32),
                        pltpu.VMEM((attn_w // HEAD_DIM, SUBLANE, tq), f32),
                        pltpu.VMEM((attn_w // HEAD_DIM, SUBLANE, tq), f32),
                        pltpu.VMEM((attn_w, tq), f32),
                        pltpu.VMEM((n_grp, tq), f32)])
    return pl.pallas_call(
        kern,
        grid_spec=grid_spec,
        out_shape=jax.ShapeDtypeStruct((s, attn_w), bf16),
        compiler_params=_cp(("arbitrary",)),
        name="dsa",
    )(qblk, kblk, feat_t, feat_t, k_tok, feat_t, kidx, widx_t)


def _merge_kernel(h_ref, a_ref, b_ref, wg0_ref, wg1_ref, wo0_ref, wo1_ref, x_ref, g_ref, o_ref):
    h = h_ref[...]
    g0 = jax.nn.sigmoid(jnp.dot(h, wg0_ref[...], preferred_element_type=f32))
    g1 = jax.nn.sigmoid(jnp.dot(h, wg1_ref[...], preferred_element_type=f32))
    ya = jnp.dot(a_ref[...], wo0_ref[...], preferred_element_type=f32)
    yb = jnp.dot(b_ref[...], wo1_ref[...], preferred_element_type=f32)
    o_ref[...] = x_ref[...] + g_ref[...] * (g0 * ya + g1 * yb)


def _merge(h, a_out, b_out, w_gate, w_out, x, gate1):
    s, d = x.shape
    cw = a_out.shape[1]
    assert b_out.shape[1] == cw
    tm, tn = _tile(s, TM_MERGE), _tile(d, TN_MERGE)
    nj = d // tn
    return pl.pallas_call(
        _merge_kernel,
        grid=(s // tm, nj),
        in_specs=[pl.BlockSpec((tm, d), lambda i, j: (i, 0)),
                  pl.BlockSpec((tm, cw), lambda i, j: (i, 0)),
                  pl.BlockSpec((tm, cw), lambda i, j: (i, 0)),
                  pl.BlockSpec((d, tn), lambda i, j: (0, j)),
                  pl.BlockSpec((d, tn), lambda i, j: (0, j + nj)),
                  pl.BlockSpec((cw, tn), lambda i, j: (0, j)),
                  pl.BlockSpec((cw, tn), lambda i, j: (1, j)),
                  pl.BlockSpec((tm, tn), lambda i, j: (i, j)),
                  pl.BlockSpec((1, tn), lambda i, j: (0, j))],
        out_specs=pl.BlockSpec((tm, tn), lambda i, j: (i, j)),
        out_shape=jax.ShapeDtypeStruct((s, d), f32),
        compiler_params=_cp(("arbitrary", "arbitrary")),
        name="merge",
    )(h, a_out, b_out, w_gate, w_gate, w_out, w_out, x, gate1)


def _top_sorted(x, k):
    n = x.shape[0]
    rows = lax.broadcasted_iota(i32, x.shape, 0).astype(f32)
    cur, tops = x, []
    for _ in range(k):
        m = jnp.max(cur, axis=0, keepdims=True)
        first = jnp.min(jnp.where(cur == m, rows, float(n)), axis=0, keepdims=True)
        cur = jnp.where(rows == first, -jnp.inf, cur)
        tops.append(m)
    return tops


def _route_kernel(h_ref, wq_ref, sk_ref, a_ref, b_ref, ea_ref, eb_ref, thr_ref, *, half, topk):
    qt = jnp.dot(wq_ref[...], h_ref[...], preferred_element_type=f32).astype(bf16)
    a = jnp.dot(sk_ref[0, 0], qt[:half], preferred_element_type=f32)
    b = jnp.dot(sk_ref[0, 1], qt[half:], preferred_element_type=f32)
    ta = _top_sorted(a, topk)
    tb = jnp.concatenate(_top_sorted(b, topk), axis=0)
    n_q = lambda p: -(-(topk // (p + 1)) // SUBLANE) * SUBLANE
    cand = jnp.concatenate([t + tb[:n_q(p)] for p, t in enumerate(ta)], axis=0)
    best = _top_sorted(cand, topk)
    thr = best[-1]
    z = jnp.ones_like(thr)
    for t in best[1:]:
        z = z + jnp.exp(t - best[0])
    a_ref[0] = a
    b_ref[0] = b
    ea_ref[0] = jnp.exp(a - ta[0])
    eb_ref[0] = jnp.exp(b - tb[0:1]) / z
    thr_ref[0] = jnp.broadcast_to(thr, thr_ref.shape[1:])


def _route(h2t, wq_t, sub_keys):
    d, s = h2t.shape
    heads, _, n_keys, half = sub_keys.shape
    tt = _tile(s, TT_ROUTE)
    big = pl.BlockSpec((1, n_keys, tt), lambda t, h: (h, 0, t))
    shp = jax.ShapeDtypeStruct((heads, n_keys, s), f32)
    return pl.pallas_call(
        functools.partial(_route_kernel, half=half, topk=PEER_TOPK),
        grid=(s // tt, heads),
        in_specs=[pl.BlockSpec((d, tt), lambda t, h: (0, t)),
                  pl.BlockSpec((2 * half, d), lambda t, h: (h, 0)),
                  pl.BlockSpec((1, 2, n_keys, half), lambda t, h: (h, 0, 0, 0))],
        out_specs=[big, big, big, big, pl.BlockSpec((1, SUBLANE, tt), lambda t, h: (h, 0, t))],
        out_shape=[shp, shp, shp, shp, jax.ShapeDtypeStruct((heads, SUBLANE, s), f32)],
        compiler_params=_cp(("arbitrary", "arbitrary")),
        name="route",
    )(h2t, wq_t, sub_keys)


def _peer_kernel(h_ref, u_ref, v_ref, a_ref, b_ref, ea_ref, eb_ref, thr_ref, o_ref, at_ref, g_ref, wt_ref,
                 *, ec, sub, heads, n_keys, lw):
    e = pl.program_id(1)
    d, tt = h_ref.shape
    nsub = ec // sub
    pieces = [(ii, lh) for ii in range(sub // n_keys) for lh in range(tt // lw)]
    dn = d // len(pieces)

    @pl.when(e == 0)
    def _():
        o_ref[...] = jnp.zeros(o_ref.shape, f32)

    def gate_piece(c, ii, lh):
        row = e * (ec // n_keys) + c * (sub // n_keys) + ii
        rs = slice(c * sub + ii * n_keys, c * sub + (ii + 1) * n_keys)
        ls = slice(lh * lw, (lh + 1) * lw)
        g = jnp.zeros((n_keys, lw), f32)
        for hd in range(heads):
            val = a_ref[hd, pl.ds(row, 1), :][:, ls] + b_ref[hd, :, ls]
            g = g + jnp.where(val >= thr_ref[hd, 0:1, ls],
                              ea_ref[hd, pl.ds(row, 1), :][:, ls] * eb_ref[hd, :, ls], 0.0)
        g_ref[rs, ls] = g

    def combine(c):
        for ii, lh in pieces:
            rs = slice(c * sub + ii * n_keys, c * sub + (ii + 1) * n_keys)
            ls = slice(lh * lw, (lh + 1) * lw)
            at = at_ref[rs, ls]
            act = 0.5 * at * (1.0 + lax.erf(at * np.float32(np.sqrt(0.5))))
            wt_ref[ls, rs] = (g_ref[rs, ls] * act).T.astype(bf16)

    def out_chunk(c, k):
        cs = slice(c * sub, (c + 1) * sub)
        ns = slice(k * dn, (k + 1) * dn)
        o_ref[:, ns] += jnp.dot(wt_ref[:, cs], v_ref[cs, ns], preferred_element_type=f32)

    h = h_ref[...]
    for p in pieces:
        gate_piece(0, *p)
    for c in range(nsub):
        cs = slice(c * sub, (c + 1) * sub)
        at_ref[cs, :] = jnp.dot(u_ref[cs, :], h, preferred_element_type=f32)
    for c in range(nsub):
        combine(c)
        for k, p in enumerate(pieces):
            out_chunk(c, k)
            if c + 1 < nsub:
                gate_piece(c + 1, *p)


def _peer(h2t, u, v, a, b, ea, eb, thr):
    d, s = h2t.shape
    n_exp = u.shape[0]
    heads, n_keys, _ = a.shape
    tt, ec = _tile(s, TT_PEER), _tile(n_exp, EC_PEER)
    sub = min(ec, SUB_PEER)
    assert ec % sub == 0 and sub % n_keys == 0
    once = pl.Buffered(1)
    tab = pl.BlockSpec((heads, n_keys, tt), lambda t, e: (0, 0, t), pipeline_mode=once)
    return pl.pallas_call(
        functools.partial(_peer_kernel, ec=ec, sub=sub, heads=heads, n_keys=n_keys, lw=LANE),
        grid=(s // tt, n_exp // ec),
        in_specs=[pl.BlockSpec((d, tt), lambda t, e: (0, t), pipeline_mode=once),
                  pl.BlockSpec((ec, d), lambda t, e: (e, 0)),
                  pl.BlockSpec((ec, d), lambda t, e: (e, 0)),
                  tab, tab, tab, tab,
                  pl.BlockSpec((heads, SUBLANE, tt), lambda t, e: (0, 0, t), pipeline_mode=once)],
        out_specs=pl.BlockSpec((tt, d), lambda t, e: (t, 0)),
        out_shape=jax.ShapeDtypeStruct((s, d), f32),
        scratch_shapes=[pltpu.VMEM((ec, tt), f32), pltpu.VMEM((ec, tt), f32), pltpu.VMEM((tt, ec), bf16)],
        compiler_params=_cp(("arbitrary", "arbitrary")),
        name="peer",
    )(h2t, u, v, a, b, ea, eb, thr)


def _final_kernel(x_ref, p_ref, g2_ref, fg_ref, o_ref):
    x = x_ref[...] + g2_ref[...] * p_ref[...]
    o_ref[...] = x * lax.rsqrt(jnp.mean(x * x, axis=-1, keepdims=True) + EPS) * fg_ref[...]


def _final(x1, p, gate2, fg):
    s, d = x1.shape
    tm = _tile(s, TM_FINAL)
    row = pl.BlockSpec((1, d), lambda i: (0, 0))
    blk = pl.BlockSpec((tm, d), lambda i: (i, 0))
    return pl.pallas_call(
        _final_kernel,
        grid=(s // tm,),
        in_specs=[blk, blk, row, row],
        out_specs=blk,
        out_shape=jax.ShapeDtypeStruct((s, d), f32),
        compiler_params=_cp(("arbitrary",)),
        name="final",
    )(x1, p, gate2, fg)


def _rope_tables(positions, dim):
    inv_freq = ROPE_THETA ** (-jnp.arange(0, dim, 2, dtype=f32) / dim)
    ang = positions.astype(f32)[:, None] * inv_freq
    return jnp.cos(ang), jnp.sin(ang)


def kernel(x, c, positions, w_ada, b_ada, norm_mix_g, norm_ffn_g, w_in, conv_w, conv_b, conv_ln_g, conv_ln_b,
           k_idx_ln_g, k_idx_ln_b, w_out, peer_w_q, peer_sub_keys, peer_u, peer_v, final_norm_g):
    bsz, s, d = x.shape
    depth = w_ada.shape[0]
    assert bsz == 1 and depth == 1, "one sequence, one layer"
    cw = conv_w.shape[-1]
    aw = N_ATTN_HEADS * HEAD_DIM
    iw = N_IDX_HEADS * IDX_HEAD_DIM
    assert cw == aw
    x2d = x.reshape(s, d)
    pos = positions.reshape(s)

    w_in2 = w_in.reshape(d, -1)
    o_q, o_k, o_v, o_qi = 2 * cw, 2 * cw + aw, 2 * cw + 2 * aw, 2 * cw + 3 * aw
    o_ki = o_qi + iw
    o_wi = o_ki + IDX_HEAD_DIM
    o_g = o_wi + N_IDX_HEADS
    w_glu = w_in2[:, :o_q].astype(bf16)
    w_feat_t = jnp.concatenate([w_in2[:, o_qi:o_ki].astype(bf16), w_in2[:, o_q:o_k].astype(bf16),
                                w_in2[:, o_v:o_qi].astype(bf16)], axis=1).T
    w_k = w_in2[:, o_k:o_v].astype(bf16)
    w_kw = w_in2[:, o_ki:o_ki + 2 * LANE].astype(bf16)
    w_gate = w_in2[:, o_g:].astype(bf16)
    w_out2 = w_out.reshape(cw + aw, d).astype(bf16)
    wq_t = peer_w_q.reshape(d, -1).astype(bf16).T
    sub_keys = peer_sub_keys.reshape(peer_sub_keys.shape[1:]).astype(bf16)
    u_exp = peer_u.reshape(-1, d).astype(bf16)
    v_exp = peer_v.reshape(-1, d).astype(bf16)

    cos_f, sin_f = _rope_tables(pos, HEAD_DIM)
    cos_p, sin_p = _rope_tables(pos, IDX_ROPE_DIM)
    cos_tok = jnp.concatenate([cos_f, cos_f], axis=1)
    sin_tok = jnp.concatenate([-sin_f, sin_f], axis=1)
    zeros_p = jnp.zeros_like(sin_p)
    pad = jnp.zeros((s, LANE - IDX_ROPE_DIM), f32)
    c_tok = jnp.concatenate([cos_p, cos_p, pad + 1.0], axis=1)
    s1_tok = jnp.concatenate([zeros_p, sin_p, pad], axis=1)
    s2_tok = jnp.concatenate([-sin_p, zeros_p, pad], axis=1)

    ada = _ada(jnp.broadcast_to(c.reshape(1, d), (SUBLANE, d)), w_ada.reshape(d, -1), b_ada.reshape(1, -1))[0:1]
    shift1, scale1, gate1, shift2, scale2, gate2 = jnp.split(ada, 6, axis=-1)

    h = _norm_mod(x2d, norm_mix_g.reshape(1, d), scale1, shift1)
    u = _glu(h, w_glu, cw)
    a_out = _conv(u, conv_w.reshape(-1, cw), conv_b.reshape(1, cw), conv_ln_g.reshape(1, cw), conv_ln_b.reshape(1, cw))
    feat_t = _proj_t(w_feat_t, h, cos_f.T, sin_f.T, cos_p.T, sin_p.T, iw, aw, HEAD_DIM ** -0.5 * LOG2E)
    k_tok = _proj_k(h, w_k, cos_tok, sin_tok)
    w_scale = float(N_IDX_HEADS ** -0.5 * IDX_HEAD_DIM ** -0.5)
    kidx, widx_t = _kw(h, w_kw, k_idx_ln_g.reshape(1, -1), k_idx_ln_b.reshape(1, -1), c_tok, s1_tok, s2_tok, w_scale)
    b_out = _dsa(feat_t, k_tok, kidx, widx_t, iw, aw, min(TOPK_MAX, s // 4))
    x1 = _merge(h, a_out, b_out, w_gate, w_out2, x2d, gate1)

    h2t = _norm_mod(x1, norm_ffn_g.reshape(1, d), scale2, shift2, transposed=True)
    a, b, ea, eb, thr = _route(h2t, wq_t, sub_keys)
    p = _peer(h2t, u_exp, v_exp, a, b, ea, eb, thr)
    return _final(x1, p, gate2, final_norm_g.reshape(1, d)).reshape(bsz, s, d)
```
